```python
import math
import jax, jax.numpy as jnp
from jax import lax
import numpy as np


D_MODEL = 1024
BATCH = 16
SEQ = 4096
DEPTH = 1

CHUNK = 64
Q_BLOCK = 128
MEM_LEN = 256
DA_HEADS = 4
DA_DH = 64
DA_WIDTH = DA_HEADS * 2 * DA_DH
RW_HEADS = 4
RW_DH = 64
RW_WIDTH = RW_HEADS * RW_DH
RW_DECAY_RANK = 64
RW_A_RANK = 64
RW_GATE_RANK = 128
RW_COLS = 3 * RW_WIDTH + RW_DECAY_RANK + RW_A_RANK + RW_GATE_RANK
XA_HEADS = 4
XA_DH = 64
XA_WIDTH = XA_HEADS * XA_DH
N_BRANCH = 3
PROJ_COLS = 3 * DA_WIDTH + RW_COLS + XA_WIDTH + N_BRANCH * D_MODEL
D_FF = 2816
ROPE_THETA = 500000.0
ROT_DIM = DA_DH // 4
EPS = 1e-6
RW_GN_EPS = 64e-5

kernel_name = 'hybrid_diffattn_rwkv7_memxattn_macaron'


def rms_norm(x, gain):
    xf = x.astype(jnp.float32)
    y = xf * lax.rsqrt(jnp.mean(xf * xf, axis=-1, keepdims=True) + EPS)
    return (y * gain.astype(jnp.float32)).astype(x.dtype)


def swiglu_half_step(x, norm_g, w1, w3, w2):
    h = rms_norm(x, norm_g)
    return x + 0.5 * ((jax.nn.silu(h @ w1) * (h @ w3)) @ w2)


def apply_partial_rope(t, positions):
    half = ROT_DIM // 2
    inv_freq = jnp.power(jnp.float32(ROPE_THETA), -jnp.arange(half, dtype=jnp.float32) * (2.0 / ROT_DIM))
    ang = positions.astype(jnp.float32)[..., None] * inv_freq
    cos = jnp.cos(ang)[:, :, None, :]
    sin = jnp.sin(ang)[:, :, None, :]
    tr = t[..., :ROT_DIM].astype(jnp.float32)
    t1, t2 = tr[..., :half], tr[..., half:]
    rot = jnp.concatenate([t1 * cos - t2 * sin, t2 * cos + t1 * sin], axis=-1).astype(t.dtype)
    return jnp.concatenate([rot, t[..., ROT_DIM:]], axis=-1)


def diff_attention(q, k, v, lam, positions):
    B, S, _ = q.shape
    q = apply_partial_rope(q.reshape(B, S, 2 * DA_HEADS, DA_DH), positions)
    k = apply_partial_rope(k.reshape(B, S, 2 * DA_HEADS, DA_DH), positions)
    q = q.reshape(B, S, DA_HEADS, 2, DA_DH).transpose(0, 2, 3, 1, 4)
    k = k.reshape(B, S, DA_HEADS, 2, DA_DH).transpose(0, 2, 3, 1, 4)
    v = v.reshape(B, S, DA_HEADS, 2 * DA_DH).transpose(0, 2, 1, 3)
    scale = DA_DH ** -0.5
    outs = []
    for i in range(S // Q_BLOCK):
        q0 = i * Q_BLOCK
        kv_len = q0 + Q_BLOCK
        s = jnp.einsum('bhmqd,bhmkd->bhmqk', q[:, :, :, q0:kv_len], k[:, :, :, :kv_len]).astype(jnp.float32) * scale
        q_chunk = (q0 + jnp.arange(Q_BLOCK)) // CHUNK
        k_chunk = jnp.arange(kv_len) // CHUNK
        allowed = k_chunk[None, :] <= q_chunk[:, None]
        p = jax.nn.softmax(jnp.where(allowed, s, -jnp.inf), axis=-1)
        attn = p[:, :, 0] - lam * p[:, :, 1]
        outs.append(jnp.einsum('bhqk,bhkd->bhqd', attn.astype(v.dtype), v[:, :, :kv_len]))
    return jnp.concatenate(outs, axis=2)


def token_shift(u, mu):
    prev = jnp.pad(u, ((0, 0), (1, 0), (0, 0)))[:, :-1]
    return u + (prev - u) * mu


def rwkv7_recurrence(r, w, k, v, a_vec, b_vec):
    B, S, H, N = r.shape

    def step(state, inp):
        r_t, w_t, k_t, v_t, a_t, b_t = inp
        sa = jnp.einsum('bhij,bhj->bhi', state, a_t)
        state = state * w_t[:, :, None, :] + sa[..., None] * b_t[:, :, None, :] + v_t[..., None] * k_t[:, :, None, :]
        return state, jnp.einsum('bhij,bhj->bhi', state, r_t)

    seq_first = lambda t: jnp.swapaxes(t, 0, 1)
    xs = (seq_first(r), seq_first(w), seq_first(k), seq_first(v), seq_first(a_vec), seq_first(b_vec))
    _, ys = lax.scan(step, jnp.zeros((B, H, N, N), jnp.float32), xs)
    return jnp.swapaxes(ys, 0, 1)


def rwkv7_time_mix(u, mu, w0, w2, a0, a2, g2, k_k, k_a, r_k, ln_w, ln_b):
    B, S, _ = u.shape
    u = token_shift(u, mu)
    i1 = RW_WIDTH
    i2 = 2 * RW_WIDTH
    i3 = 3 * RW_WIDTH
    i4 = i3 + RW_DECAY_RANK
    i5 = i4 + RW_A_RANK
    r, k, v, dw, da, dg = jnp.split(u, [i1, i2, i3, i4, i5], axis=-1)
    w_log = -jax.nn.softplus(-(w0 + jnp.tanh(dw) @ w2).astype(jnp.float32)) - 0.5
    decay = jnp.exp(-jnp.exp(w_log))
    a = jax.nn.sigmoid((a0 + da @ a2).astype(jnp.float32))
    g = (jax.nn.sigmoid(dg) @ g2).astype(jnp.float32)
    heads = lambda t: t.astype(jnp.float32).reshape(B, S, RW_HEADS, RW_DH)
    per_head = lambda p: p.astype(jnp.float32).reshape(RW_HEADS, RW_DH)
    r_h, k_h, v_h, a_h, w_h = heads(r), heads(k), heads(v), heads(a), heads(decay)
    kk = k_h * per_head(k_k)
    kk = kk / jnp.maximum(jnp.sqrt(jnp.sum(kk * kk, axis=-1, keepdims=True)), 1e-12)
    k_h = k_h * (1.0 + (a_h - 1.0) * per_head(k_a))
    y = rwkv7_recurrence(r_h, w_h, k_h, v_h, -kk, kk * a_h)
    mean = jnp.mean(y, axis=-1, keepdims=True)
    var = jnp.mean(jnp.square(y - mean), axis=-1, keepdims=True)
    y = (y - mean) * lax.rsqrt(var + RW_GN_EPS) * per_head(ln_w) + per_head(ln_b)
    y = y + jnp.sum(r_h * k_h * per_head(r_k), axis=-1, keepdims=True) * v_h
    return (y.reshape(B, S, RW_WIDTH) * g).astype(u.dtype)


def memory_attention(q, mem_h, w_kv):
    B, S, _ = q.shape
    M = mem_h.shape[1]
    k, v = jnp.split(mem_h @ w_kv, 2, axis=-1)
    q = q.reshape(B, S, XA_HEADS, XA_DH)
    k = k.reshape(B, M, XA_HEADS, XA_DH)
    v = v.reshape(B, M, XA_HEADS, XA_DH)
    s = jnp.einsum('bqhd,bkhd->bhqk', q, k).astype(jnp.float32) * (XA_DH ** -0.5)
    p = jax.nn.softmax(s, axis=-1)
    o = jnp.einsum('bhqk,bkhd->bqhd', p.astype(v.dtype), v)
    return o.reshape(B, S, XA_WIDTH)


def setup_inputs(seed: int = 0) -> dict:
    key = jax.random.key(seed)
    ks = iter(jax.random.split(key, 48))
    nrm = lambda shape, scale: jax.random.normal(next(ks), shape, jnp.float32) * scale
    gain = lambda shape: 1.0 + 0.02 * jax.random.normal(next(ks), shape, jnp.float32)
    L = DEPTH
    offsets = jax.random.randint(next(ks), (BATCH,), 0, 64) * CHUNK
    positions = (offsets[:, None] + jnp.arange(SEQ)[None, :]).astype(jnp.int32)
    return {
        'x': nrm((BATCH, SEQ, D_MODEL), 1.0),
        'mem': nrm((BATCH, MEM_LEN, D_MODEL), 1.0),
        'positions': positions,
        'ffn1_norm': gain((L, D_MODEL)),
        'ffn1_w1': nrm((L, D_MODEL, D_FF), D_MODEL ** -0.5),
        'ffn1_w3': nrm((L, D_MODEL, D_FF), D_MODEL ** -0.5),
        'ffn1_w2': nrm((L, D_FF, D_MODEL), D_FF ** -0.5),
        'mix_norm': gain((L, D_MODEL)),
        'w_in': nrm((L, D_MODEL, PROJ_COLS), D_MODEL ** -0.5),
        'gate_bias': nrm((L, N_BRANCH * D_MODEL), 0.02),
        'da_lambda_q1': nrm((L, DA_DH), 0.1),
        'da_lambda_k1': nrm((L, DA_DH), 0.1),
        'da_lambda_q2': nrm((L, DA_DH), 0.1),
        'da_lambda_k2': nrm((L, DA_DH), 0.1),
        'da_subln': gain((L, 2 * DA_DH)),
        'rw_mu': jax.random.uniform(next(ks), (L, RW_COLS), jnp.float32),
        'rw_w0': -1.0 - 5.0 * jax.random.uniform(next(ks), (L, RW_WIDTH), jnp.float32),
        'rw_w2': nrm((L, RW_DECAY_RANK, RW_WIDTH), 0.1 * RW_DECAY_RANK ** -0.5),
        'rw_a0': nrm((L, RW_WIDTH), 0.1),
        'rw_a2': nrm((L, RW_A_RANK, RW_WIDTH), 0.1 * RW_A_RANK ** -0.5),
        'rw_g2': nrm((L, RW_GATE_RANK, RW_WIDTH), RW_GATE_RANK ** -0.5),
        'rw_k_k': 0.85 + nrm((L, RW_WIDTH), 0.02),
        'rw_k_a': gain((L, RW_WIDTH)),
        'rw_r_k': nrm((L, RW_WIDTH), 0.1),
        'rw_ln_w': gain((L, RW_WIDTH)),
        'rw_ln_b': nrm((L, RW_WIDTH), 0.02),
        'mem_norm': gain((L, D_MODEL)),
        'w_mem_kv': nrm((L, D_MODEL, 2 * XA_WIDTH), D_MODEL ** -0.5),
        'w_up_a': nrm((L, DA_WIDTH, D_MODEL), DA_WIDTH ** -0.5),
        'w_up_b': nrm((L, RW_WIDTH, D_MODEL), RW_WIDTH ** -0.5),
        'w_up_c': nrm((L, XA_WIDTH, D_MODEL), XA_WIDTH ** -0.5),
        'w_out': nrm((L, D_MODEL, D_MODEL), D_MODEL ** -0.5),
        'ffn2_norm': gain((L, D_MODEL)),
        'ffn2_w1': nrm((L, D_MODEL, D_FF), D_MODEL ** -0.5),
        'ffn2_w3': nrm((L, D_MODEL, D_FF), D_MODEL ** -0.5),
        'ffn2_w2': nrm((L, D_FF, D_MODEL), D_FF ** -0.5),
        'final_norm': gain((D_MODEL,)),
    }


def reference(x, mem, positions, ffn1_norm, ffn1_w1, ffn1_w3, ffn1_w2, mix_norm, w_in, gate_bias,
              da_lambda_q1, da_lambda_k1, da_lambda_q2, da_lambda_k2, da_subln,
              rw_mu, rw_w0, rw_w2, rw_a0, rw_a2, rw_g2, rw_k_k, rw_k_a, rw_r_k, rw_ln_w, rw_ln_b,
              mem_norm, w_mem_kv, w_up_a, w_up_b, w_up_c, w_out,
              ffn2_norm, ffn2_w1, ffn2_w3, ffn2_w2, final_norm):
    B, S, D = x.shape
    c1 = DA_WIDTH
    c2 = 2 * DA_WIDTH
    c3 = 3 * DA_WIDTH
    c4 = c3 + RW_COLS
    c5 = c4 + XA_WIDTH
    for l in range(DEPTH):
        x = swiglu_half_step(x, ffn1_norm[l], ffn1_w1[l], ffn1_w3[l], ffn1_w2[l])
        h = rms_norm(x, mix_norm[l])
        proj = h @ w_in[l]
        q_da, k_da, v_da, u_rw, q_x, gate_lin = jnp.split(proj, [c1, c2, c3, c4, c5], axis=-1)
        lam_init = 0.8 - 0.6 * math.exp(-0.3 * l)
        lam = (jnp.exp(jnp.sum(da_lambda_q1[l] * da_lambda_k1[l]).astype(jnp.float32))
               - jnp.exp(jnp.sum(da_lambda_q2[l] * da_lambda_k2[l]).astype(jnp.float32)) + lam_init)
        o_a = diff_attention(q_da, k_da, v_da, lam, positions)
        o_a = rms_norm(o_a, da_subln[l]) * (1.0 - lam_init)
        o_a = o_a.transpose(0, 2, 1, 3).reshape(B, S, DA_WIDTH)
        o_b = rwkv7_time_mix(u_rw, rw_mu[l], rw_w0[l], rw_w2[l], rw_a0[l], rw_a2[l], rw_g2[l],
                             rw_k_k[l], rw_k_a[l], rw_r_k[l], rw_ln_w[l], rw_ln_b[l])
        o_c = memory_attention(q_x, rms_norm(mem, mem_norm[l]), w_mem_kv[l])
        gates = jax.nn.sigmoid((gate_lin + gate_bias[l]).astype(jnp.float32)).astype(x.dtype)
        gates = gates.reshape(B, S, N_BRANCH, D)
        merged = (gates[:, :, 0] * (o_a @ w_up_a[l])
                  + gates[:, :, 1] * (o_b @ w_up_b[l])
                  + gates[:, :, 2] * (o_c @ w_up_c[l]))
        x = x + merged @ w_out[l]
        x = swiglu_half_step(x, ffn2_norm[l], ffn2_w1[l], ffn2_w3[l], ffn2_w2[l])
    return rms_norm(x, final_norm)
```

```python
import functools
import math

import jax
import jax.numpy as jnp
from jax import lax
from jax.experimental import pallas as pl
from jax.experimental.pallas import tpu as pltpu

F32 = jnp.float32
BF16 = jnp.bfloat16

CHUNK = 64
DA_HEADS = 4
DA_DH = 64
DA_WIDTH = DA_HEADS * 2 * DA_DH
RW_HEADS = 4
RW_DH = 64
RW_WIDTH = RW_HEADS * RW_DH
RW_DECAY_RANK = 64
RW_A_RANK = 64
RW_GATE_RANK = 128
RW_COLS = 3 * RW_WIDTH + RW_DECAY_RANK + RW_A_RANK + RW_GATE_RANK
RW_CHUNK = 64
XA_HEADS = 4
XA_DH = 64
XA_WIDTH = XA_HEADS * XA_DH
ROPE_THETA = 500000.0
ROT_DIM = DA_DH // 4
ROT_HALF = ROT_DIM // 2
EPS = 1e-6
RW_GN_EPS = 64e-5
LANES = 128
NEG_BIG = -1e30
VMEM_LIMIT = 56 * 1024 * 1024


def _cparams(*sem):
    return pltpu.CompilerParams(dimension_semantics=sem, vmem_limit_bytes=VMEM_LIMIT)


def _resident(shape):
    return pl.BlockSpec(shape, lambda *_: (0,) * len(shape), pipeline_mode=pl.Buffered(1))


def _mm(a, b):
    return jnp.dot(a.astype(BF16), b.astype(BF16), preferred_element_type=F32)


def _mm_nt(a, b):
    return lax.dot_general(a.astype(BF16), b.astype(BF16), (((1,), (1,)), ((), ())),
                           preferred_element_type=F32)


def _mm_tn(a, b):
    return lax.dot_general(a.astype(BF16), b.astype(BF16), (((0,), (0,)), ((), ())),
                           preferred_element_type=F32)


def _mm_f32(a, m_bf16):
    hi = a.astype(BF16)
    r1 = a - hi.astype(F32)
    mid = r1.astype(BF16)
    lo = (r1 - mid.astype(F32)).astype(BF16)
    dot = functools.partial(jnp.dot, preferred_element_type=F32)
    return dot(hi, m_bf16) + dot(mid, m_bf16) + dot(lo, m_bf16)


def _mm_f32_left(m_bf16, a):
    hi = a.astype(BF16)
    r1 = a - hi.astype(F32)
    mid = r1.astype(BF16)
    lo = (r1 - mid.astype(F32)).astype(BF16)
    dot = functools.partial(jnp.dot, preferred_element_type=F32)
    return dot(m_bf16, hi) + dot(m_bf16, mid) + dot(m_bf16, lo)


def _rms(x, gain):
    return x * lax.rsqrt(jnp.mean(x * x, axis=-1, keepdims=True) + EPS) * gain


def _ffn_kernel(x_ref, gin_ref, w1_ref, w3_ref, w2_ref, gout_ref, *out_refs, ff_chunk, final):
    x = x_ref[...]
    h = _rms(x, gin_ref[...]).astype(BF16)
    d_ff = w1_ref.shape[1]
    acc = jnp.zeros(x.shape, F32)
    for c in range(d_ff // ff_chunk):
        sl = slice(c * ff_chunk, (c + 1) * ff_chunk)
        a = jnp.dot(h, w1_ref[:, sl], preferred_element_type=F32)
        b = jnp.dot(h, w3_ref[:, sl], preferred_element_type=F32)
        act = (a * jax.nn.sigmoid(a) * b).astype(BF16)
        acc = acc + jnp.dot(act, w2_ref[sl, :], preferred_element_type=F32)
    y = x + 0.5 * acc
    if final:
        out_refs[0][...] = _rms(y, gout_ref[...])
    else:
        out_refs[0][...] = y
        out_refs[1][...] = _rms(y, gout_ref[...]).astype(BF16)


def _ffn(x, gin, w1, w3, w2, gout, *, final, tm, ff_chunk):
    t, d = x.shape
    d_ff = w1.shape[1]
    row = pl.BlockSpec((tm, d), lambda i: (i, 0))
    if final:
        out_shape = jax.ShapeDtypeStruct((t, d), F32)
        out_specs = row
    else:
        out_shape = (jax.ShapeDtypeStruct((t, d), F32), jax.ShapeDtypeStruct((t, d), BF16))
        out_specs = (row, row)
    return pl.pallas_call(
        functools.partial(_ffn_kernel, ff_chunk=ff_chunk, final=final),
        grid=(t // tm,),
        in_specs=[row, _resident((1, d)), _resident((d, d_ff)), _resident((d, d_ff)),
                  _resident((d_ff, d)), _resident((1, d))],
        out_specs=out_specs,
        out_shape=out_shape,
        compiler_params=_cparams("parallel"),
        name="ffn_final" if final else "ffn_first",
    )(x, gin.reshape(1, d), w1.astype(BF16), w3.astype(BF16), w2.astype(BF16), gout.reshape(1, d))


def _rope_kernel(pos_ref, invf_ref, cos_ref, sina_ref, sinb_ref):
    pos = pos_ref[...].astype(F32)
    ang = invf_ref[...] * pos
    c = jnp.cos(ang)
    s = jnp.sin(ang)
    one = jnp.ones_like(c)
    zero = jnp.zeros_like(c)
    pad = DA_DH // ROT_HALF - 2

    def lanes(first, second, fill):
        head = [first, second] + [fill] * pad
        return jnp.concatenate(head * (LANES // DA_DH), axis=0).T

    cos_ref[...] = lanes(c, c, one)
    sina_ref[...] = lanes(-s, zero, zero)
    sinb_ref[...] = lanes(zero, s, zero)


def _rope_tables(positions, *, ts):
    t = positions.size
    pos = positions.reshape(t // ts, 1, ts)
    inv_freq = jnp.power(jnp.float32(ROPE_THETA),
                         -jnp.arange(ROT_HALF, dtype=F32) * (2.0 / ROT_DIM)).reshape(ROT_HALF, 1)
    out = jax.ShapeDtypeStruct((t, LANES), F32)
    spec = pl.BlockSpec((ts, LANES), lambda i: (i, 0))
    return pl.pallas_call(
        _rope_kernel,
        grid=(t // ts,),
        in_specs=[pl.BlockSpec((None, 1, ts), lambda i: (i, 0, 0)), _resident((ROT_HALF, 1))],
        out_specs=(spec, spec, spec),
        out_shape=(out, out, out),
        compiler_params=_cparams("parallel"),
        name="rope_tables",
    )(pos, inv_freq)


def _in_proj_kernel(h_ref, w_ref, cos_ref, sina_ref, sinb_ref, gb_ref,
                    q_ref, k_ref, v_ref, u_ref, qx_ref, g_ref, *, d_model):
    h = h_ref[...]
    cos_t = cos_ref[...]
    sin_a = sina_ref[...]
    sin_b = sinb_ref[...]

    def seg(lo, width):
        return jnp.dot(h, w_ref[:, lo:lo + width], preferred_element_type=F32)

    def rope(t):
        nxt = pltpu.roll(t, LANES - ROT_HALF, axis=1)
        prv = pltpu.roll(t, ROT_HALF, axis=1)
        return t * cos_t + nxt * sin_a + prv * sin_b

    c = 0
    for j in range(DA_WIDTH // LANES):
        t = seg(c + j * LANES, LANES)
        q_ref[:, j * LANES:(j + 1) * LANES] = (rope(t) * (DA_DH ** -0.5)).astype(BF16)
    c += DA_WIDTH
    for j in range(DA_WIDTH // LANES):
        t = seg(c + j * LANES, LANES)
        k_ref[:, j * LANES:(j + 1) * LANES] = rope(t).astype(BF16)
    c += DA_WIDTH
    v_ref[...] = seg(c, DA_WIDTH).astype(BF16)
    c += DA_WIDTH
    u_ref[...] = seg(c, RW_COLS)
    c += RW_COLS
    qx_ref[...] = (seg(c, XA_WIDTH) * (XA_DH ** -0.5)).astype(BF16)
    c += XA_WIDTH
    for j in range(3):
        gl = seg(c + j * d_model, d_model) + gb_ref[:, j * d_model:(j + 1) * d_model]
        g_ref[:, j * d_model:(j + 1) * d_model] = jax.nn.sigmoid(gl).astype(BF16)


def _in_proj(h, w_in, cos_t, sin_a, sin_b, gate_bias, *, tm):
    t, d = h.shape
    cols = w_in.shape[1]
    row = lambda w: pl.BlockSpec((tm, w), lambda i: (i, 0))
    shapes = ((DA_WIDTH, BF16), (DA_WIDTH, BF16), (DA_WIDTH, BF16), (RW_COLS, F32),
              (XA_WIDTH, BF16), (3 * d, BF16))
    return pl.pallas_call(
        functools.partial(_in_proj_kernel, d_model=d),
        grid=(t // tm,),
        in_specs=[row(d), _resident((d, cols)), row(LANES), row(LANES), row(LANES),
                  _resident((1, 3 * d))],
        out_specs=tuple(row(w) for w, _ in shapes),
        out_shape=tuple(jax.ShapeDtypeStruct((t, w), dt) for w, dt in shapes),
        compiler_params=_cparams("parallel"),
        name="in_proj",
    )(h, w_in.astype(BF16), cos_t, sin_a, sin_b, gate_bias.reshape(1, 3 * d))


def _lambda_kernel(q1_ref, k1_ref, q2_ref, k2_ref, out_ref, *, lam_init):
    l1 = jnp.exp(jnp.sum(q1_ref[...] * k1_ref[...], axis=-1, keepdims=True))
    l2 = jnp.exp(jnp.sum(q2_ref[...] * k2_ref[...], axis=-1, keepdims=True))
    out_ref[...] = jnp.broadcast_to(l1 - l2 + lam_init, out_ref.shape)


def _lambda(q1, k1, q2, k2, lam_init):
    vec = lambda a: a.reshape(1, DA_DH)
    out = pl.pallas_call(
        functools.partial(_lambda_kernel, lam_init=lam_init),
        out_shape=jax.ShapeDtypeStruct((8, LANES), F32),
        name="da_lambda",
    )(vec(q1), vec(k1), vec(q2), vec(k2))
    return out[0, :1]


def _diff_attn_kernel(lam_ref, q_ref, k_ref, v_ref, gain_ref, o_ref, *, tq, out_scale):
    i = pl.program_id(2)
    lam = lam_ref[0]
    q = q_ref[...]
    first = lax.broadcasted_iota(jnp.int32, (1, LANES), 1) < DA_DH
    zero = jnp.zeros_like(q)
    qm = (jnp.where(first, q, zero), jnp.where(first, zero, q))

    def update(s, carry, v):
        m, l, acc = carry
        m_new = jnp.maximum(m, jnp.max(s, axis=-1, keepdims=True))
        alpha = jnp.exp(m - m_new)
        p = jnp.exp(s - m_new)
        l_new = alpha * l + jnp.sum(p, axis=-1, keepdims=True)
        acc_new = alpha * acc + jnp.dot(p.astype(BF16), v, preferred_element_type=F32)
        return m_new, l_new, acc_new

    def step(j, carry, mask):
        start = pl.multiple_of(j * tq, tq)
        k = k_ref[pl.ds(start, tq), :]
        v = v_ref[pl.ds(start, tq), :]
        out = []
        for m_idx in range(2):
            s = _mm_nt(qm[m_idx], k)
            if mask is not None:
                s = jnp.where(mask, s, NEG_BIG)
            out.append(update(s, carry[m_idx], v))
        return tuple(out)

    init_one = (jnp.full((tq, 1), NEG_BIG, F32), jnp.zeros((tq, 1), F32),
                jnp.zeros((tq, LANES), F32))
    carry = lax.fori_loop(0, i, lambda j, c: step(j, c, None), (init_one, init_one))
    rows = lax.broadcasted_iota(jnp.int32, (tq, tq), 0) // CHUNK
    cols = lax.broadcasted_iota(jnp.int32, (tq, tq), 1) // CHUNK
    (_, l1, a1), (_, l2, a2) = step(i, carry, cols <= rows)
    o = a1 / l1 - lam * (a2 / l2)
    o_ref[...] = (_rms(o, gain_ref[...]) * out_scale).astype(BF16)


def _diff_attn(lam, q, k, v, gain, *, batch, seq, tq, out_scale):
    nq = seq // tq
    qspec = pl.BlockSpec((tq, LANES), lambda b, h, i: (b * nq + i, h))
    kvspec = pl.BlockSpec((seq, LANES), lambda b, h, i: (b, h))
    return pl.pallas_call(
        functools.partial(_diff_attn_kernel, tq=tq, out_scale=out_scale),
        grid=(batch, DA_HEADS, nq),
        in_specs=[pl.BlockSpec(memory_space=pltpu.SMEM), qspec, kvspec, kvspec,
                  _resident((1, LANES))],
        out_specs=qspec,
        out_shape=jax.ShapeDtypeStruct(q.shape, BF16),
        compiler_params=_cparams("parallel", "parallel", "arbitrary"),
        name="diff_attn",
    )(lam, q, k, v, gain.reshape(1, LANES))


def _head_mask(h, width=RW_WIDTH):
    lane = lax.broadcasted_iota(jnp.int32, (1, width), 1)
    return (lane // RW_DH) == h


def _stack_heads(x):
    zero = jnp.zeros_like(x)
    return jnp.concatenate([jnp.where(_head_mask(h), x, zero) for h in range(RW_HEADS)], axis=0)


def _rwkv_kernel(u_ref, mu_ref, wlr_ref, wab_ref, g2_ref, kk_ref, ka_ref, rk_ref,
                 lnw_ref, lnb_ref, o_ref,
                 carry_ref, state_ref, r_s, k_s, v_s, lw_s, cum_s, kn_s, b_s, y_s, *, ts):
    c_len = RW_CHUNK
    w = RW_WIDTH

    @pl.when(pl.program_id(1) == 0)
    def _():
        carry_ref[...] = jnp.zeros_like(carry_ref)
        state_ref[...] = jnp.zeros_like(state_ref)

    u = u_ref[...]
    row = lax.broadcasted_iota(jnp.int32, (ts, 1), 0)
    prev = jnp.where(row == 0, carry_ref[...], pltpu.roll(u, 1, axis=0))
    carry_ref[...] = u[ts - 1:ts, :]
    us = u + (prev - u) * mu_ref[...]
    r = us[:, 0:w]
    k = us[:, w:2 * w]
    v = us[:, 2 * w:3 * w]
    low = us[:, 3 * w:3 * w + LANES]
    lane = lax.broadcasted_iota(jnp.int32, (1, LANES), 1)
    low = jnp.where(lane < RW_DECAY_RANK, jnp.tanh(low), low)
    lr = jnp.dot(low, wlr_ref[...], preferred_element_type=F32,
                 precision=lax.Precision.HIGHEST)
    z = -(wab_ref[:, 0:w] + lr[:, 0:w])
    softplus = jnp.maximum(z, 0.0) + jnp.log(1.0 + jnp.exp(-jnp.abs(z)))
    lw = -jnp.exp(-softplus - 0.5)
    a = jax.nn.sigmoid(wab_ref[:, w:2 * w] + lr[:, w:2 * w])
    g = _mm(jax.nn.sigmoid(us[:, 3 * w + LANES:3 * w + LANES + RW_GATE_RANK]), g2_ref[...])

    hr = lax.broadcasted_iota(jnp.int32, (w, w), 0) // RW_DH
    hc = lax.broadcasted_iota(jnp.int32, (w, w), 1) // RW_DH
    head_ones = (hr == hc).astype(BF16)
    kk = k * kk_ref[...]
    kk = kk / jnp.maximum(jnp.sqrt(_mm_f32(kk * kk, head_ones)), 1e-12)
    k2 = k * (1.0 + (a - 1.0) * ka_ref[...])

    tr = lax.broadcasted_iota(jnp.int32, (ts, ts), 0)
    tc = lax.broadcasted_iota(jnp.int32, (ts, ts), 1)
    tri = ((tc <= tr) & (tc // c_len == tr // c_len)).astype(BF16)
    cum = _mm_f32_left(tri, lw)

    r_s[...] = r
    k_s[...] = k2
    v_s[...] = v
    lw_s[...] = lw
    cum_s[...] = cum
    kn_s[...] = kk
    b_s[...] = kk * a

    n4 = RW_HEADS * c_len
    tt = lax.broadcasted_iota(jnp.int32, (c_len, n4), 0)
    ss = lax.broadcasted_iota(jnp.int32, (c_len, n4), 1) % c_len
    strict = ss < tt
    incl = ss <= tt
    br = lax.broadcasted_iota(jnp.int32, (n4, n4), 0)
    bc = lax.broadcasted_iota(jnp.int32, (n4, n4), 1)
    blockdiag = (br // c_len) == (bc // c_len)
    eye = (br == bc).astype(F32)

    def rowblock_sum(x):
        return sum(x[h * c_len:(h + 1) * c_len, :] for h in range(RW_HEADS))

    def chunk(c, state):
        sl = pl.ds(pl.multiple_of(c * c_len, c_len), c_len)
        lw_c = lw_s[sl, :]
        cum_c = cum_s[sl, :]
        last = cum_c[c_len - 1:c_len, :]
        kn = kn_s[sl, :]
        bv = b_s[sl, :]
        k2_c = k_s[sl, :]
        v_c = v_s[sl, :]
        inv = jnp.exp(-cum_c)
        tail = jnp.exp(last - cum_c)
        a_t = -kn * jnp.exp(cum_c - lw_c)
        r_t = r_s[sl, :] * jnp.exp(cum_c)
        ar = jnp.concatenate([a_t, r_t], axis=0)
        bk = jnp.concatenate([_stack_heads(bv * inv), _stack_heads(k2_c * inv)], axis=0)
        gram = _mm_nt(ar, bk)
        zero = jnp.zeros((c_len, n4), F32)
        a_ab = jnp.where(strict, gram[:c_len, :n4], zero)
        a_ak = jnp.where(strict, gram[:c_len, n4:], zero)
        a_rb = jnp.where(incl, gram[c_len:, :n4], zero)
        a_rk = jnp.where(incl, gram[c_len:, n4:], zero)

        x = jnp.where(blockdiag, jnp.concatenate([a_ab] * RW_HEADS, axis=0), 0.0)
        t_inv = eye + x
        steps = int(math.log2(c_len)) - 1
        for _ in range(steps):
            x = _mm(x, x)
            t_inv = t_inv + _mm(t_inv, x)
        t_cat = rowblock_sum(t_inv)

        v_bd = _stack_heads(v_c)
        av = _mm(jnp.concatenate([a_ak, a_rk], axis=0), v_bd)
        xy = _mm_nt(ar, state)
        z_c = xy[:c_len, :] + av[:c_len, :]
        u_c = _mm(t_cat, _stack_heads(z_c))
        y_c = xy[c_len:, :] + av[c_len:, :] + _mm(a_rb, _stack_heads(u_c))
        y_s[sl, :] = y_c
        upd = _mm_tn(jnp.concatenate([u_c, v_c], axis=0),
                     jnp.concatenate([bv * tail, k2_c * tail], axis=0))
        return state * jnp.exp(last) + jnp.where(blockdiag, upd, 0.0)

    state_ref[...] = lax.fori_loop(0, ts // c_len, chunk, state_ref[...])

    y = y_s[...]
    mean = _mm_f32(y, head_ones) * (1.0 / RW_DH)
    yc = y - mean
    var = _mm_f32(yc * yc, head_ones) * (1.0 / RW_DH)
    yn = yc * lax.rsqrt(var + RW_GN_EPS) * lnw_ref[...] + lnb_ref[...]
    bonus = _mm_f32(r * k2 * rk_ref[...], head_ones) * v
    o_ref[...] = ((yn + bonus) * g).astype(BF16)


def _rwkv(u, mu, w0, w2, a0, a2, g2, k_k, k_a, r_k, ln_w, ln_b, *, batch, seq, ts):
    w = RW_WIDTH
    ns = seq // ts
    zeros = lambda r, c: jnp.zeros((r, c), F32)
    w_lr = jnp.concatenate([jnp.concatenate([w2, zeros(RW_DECAY_RANK, w)], axis=1),
                            jnp.concatenate([zeros(RW_A_RANK, w), a2], axis=1)], axis=0)
    w_ab = jnp.concatenate([w0, a0]).reshape(1, 2 * w)
    vec = lambda p: p.reshape(1, w)
    scratch = [pltpu.VMEM((1, RW_COLS), F32), pltpu.VMEM((w, w), F32)]
    scratch += [pltpu.VMEM((ts, w), F32) for _ in range(8)]
    return pl.pallas_call(
        functools.partial(_rwkv_kernel, ts=ts),
        grid=(batch, ns),
        in_specs=[pl.BlockSpec((ts, RW_COLS), lambda b, i: (b * ns + i, 0)),
                  _resident((1, RW_COLS)), _resident((LANES, 2 * w)), _resident((1, 2 * w)),
                  _resident((RW_GATE_RANK, w))] + [_resident((1, w))] * 5,
        out_specs=pl.BlockSpec((ts, w), lambda b, i: (b * ns + i, 0)),
        out_shape=jax.ShapeDtypeStruct((batch * seq, w), BF16),
        scratch_shapes=scratch,
        compiler_params=_cparams("parallel", "arbitrary"),
        name="rwkv7",
    )(u, mu.reshape(1, RW_COLS), w_lr, w_ab, g2.astype(BF16), vec(k_k), vec(k_a), vec(r_k),
      vec(ln_w), vec(ln_b))


def _mem_kv_kernel(mem_ref, g_ref, w_ref, kv_ref):
    h = _rms(mem_ref[...], g_ref[...]).astype(BF16)
    kv_ref[...] = jnp.dot(h, w_ref[...], preferred_element_type=F32).astype(BF16)


def _mem_kv(mem, gain, w_kv, *, batch, mem_len):
    d = mem.shape[-1]
    n = w_kv.shape[1]
    return pl.pallas_call(
        _mem_kv_kernel,
        grid=(batch,),
        in_specs=[pl.BlockSpec((mem_len, d), lambda b: (b, 0)), _resident((1, d)),
                  _resident((d, n))],
        out_specs=pl.BlockSpec((mem_len, n), lambda b: (b, 0)),
        out_shape=jax.ShapeDtypeStruct((batch * mem_len, n), BF16),
        compiler_params=_cparams("parallel"),
        name="mem_kv",
    )(mem.reshape(batch * mem_len, d), gain.reshape(1, d), w_kv.astype(BF16))


def _mem_attn_kernel(q_ref, kv_ref, o_ref):
    q = q_ref[...]
    k = kv_ref[:, 0:XA_WIDTH]
    v = kv_ref[:, XA_WIDTH:2 * XA_WIDTH]
    zero = jnp.zeros_like(q)
    out = jnp.zeros(q.shape, F32)
    for h in range(XA_HEADS):
        hm = _head_mask(h, XA_WIDTH)
        s = _mm_nt(jnp.where(hm, q, zero), k)
        p = jnp.exp(s - jnp.max(s, axis=-1, keepdims=True))
        o = jnp.dot(p.astype(BF16), v, preferred_element_type=F32)
        out = jnp.where(hm, o / jnp.sum(p, axis=-1, keepdims=True), out)
    o_ref[...] = out.astype(BF16)


def _mem_attn(qx, kv, *, batch, seq, mem_len, tq):
    nq = seq // tq
    spec = pl.BlockSpec((tq, XA_WIDTH), lambda b, i: (b * nq + i, 0))
    return pl.pallas_call(
        _mem_attn_kernel,
        grid=(batch, nq),
        in_specs=[spec, pl.BlockSpec((mem_len, 2 * XA_WIDTH), lambda b, i: (b, 0))],
        out_specs=spec,
        out_shape=jax.ShapeDtypeStruct(qx.shape, BF16),
        compiler_params=_cparams("parallel", "arbitrary"),
        name="mem_attn",
    )(qx, kv)


def _merge_kernel(x_ref, oa_ref, ob_ref, oc_ref, g_ref, wa_ref, wb_ref, wc_ref, wo_ref, o_ref):
    d = x_ref.shape[1]
    dot = functools.partial(jnp.dot, preferred_element_type=F32)
    merged = (g_ref[:, 0:d].astype(F32) * dot(oa_ref[...], wa_ref[...])
              + g_ref[:, d:2 * d].astype(F32) * dot(ob_ref[...], wb_ref[...])
              + g_ref[:, 2 * d:3 * d].astype(F32) * dot(oc_ref[...], wc_ref[...]))
    o_ref[...] = x_ref[...] + dot(merged.astype(BF16), wo_ref[...])


def _merge(x, oa, ob, oc, gates, wa, wb, wc, wo, *, tm):
    t, d = x.shape
    row = lambda w: pl.BlockSpec((tm, w), lambda i: (i, 0))
    return pl.pallas_call(
        _merge_kernel,
        grid=(t // tm,),
        in_specs=[row(d), row(DA_WIDTH), row(RW_WIDTH), row(XA_WIDTH), row(3 * d),
                  _resident(wa.shape), _resident(wb.shape), _resident(wc.shape),
                  _resident(wo.shape)],
        out_specs=row(d),
        out_shape=jax.ShapeDtypeStruct((t, d), F32),
        compiler_params=_cparams("parallel"),
        name="merge",
    )(x, oa, ob, oc, gates, wa.astype(BF16), wb.astype(BF16), wc.astype(BF16), wo.astype(BF16))


def _pick(n, pref):
    t = min(n, pref)
    while n % t:
        t //= 2
    return t


def kernel(x, mem, positions, ffn1_norm, ffn1_w1, ffn1_w3, ffn1_w2, mix_norm, w_in, gate_bias, da_lambda_q1, da_lambda_k1, da_lambda_q2, da_lambda_k2, da_subln, rw_mu, rw_w0, rw_w2, rw_a0, rw_a2, rw_g2, rw_k_k, rw_k_a, rw_r_k, rw_ln_w, rw_ln_b, mem_norm, w_mem_kv, w_up_a, w_up_b, w_up_c, w_out, ffn2_norm, ffn2_w1, ffn2_w3, ffn2_w2, final_norm):
    batch, seq, d = x.shape
    mem_len = mem.shape[1]
    depth = ffn1_norm.shape[0]
    t = batch * seq
    tm = _pick(t, 512)
    d_ff = ffn1_w1.shape[-1]
    ff_chunk = d_ff // 4 if (d_ff // 4) % LANES == 0 else d_ff
    tq = _pick(seq, 256)
    ts = _pick(seq, 512)

    cos_t, sin_a, sin_b = _rope_tables(positions, ts=_pick(t, 1024))
    xf = x.reshape(t, d)
    for l in range(depth):
        lam_init = 0.8 - 0.6 * math.exp(-0.3 * l)
        norm_after = mix_norm[l]
        x1, h = _ffn(xf, ffn1_norm[l], ffn1_w1[l], ffn1_w3[l], ffn1_w2[l], norm_after,
                     final=False, tm=tm, ff_chunk=ff_chunk)
        q, k, v, u, qx, gates = _in_proj(h, w_in[l], cos_t, sin_a, sin_b, gate_bias[l], tm=tm)
        lam = _lambda(da_lambda_q1[l], da_lambda_k1[l], da_lambda_q2[l], da_lambda_k2[l], lam_init)
        o_a = _diff_attn(lam, q, k, v, da_subln[l], batch=batch, seq=seq, tq=tq,
                         out_scale=1.0 - lam_init)
        o_b = _rwkv(u, rw_mu[l], rw_w0[l], rw_w2[l], rw_a0[l], rw_a2[l], rw_g2[l], rw_k_k[l],
                    rw_k_a[l], rw_r_k[l], rw_ln_w[l], rw_ln_b[l], batch=batch, seq=seq, ts=ts)
        kv = _mem_kv(mem, mem_norm[l], w_mem_kv[l], batch=batch, mem_len=mem_len)
        o_c = _mem_attn(qx, kv, batch=batch, seq=seq, mem_len=mem_len, tq=_pick(seq, 512))
        x2 = _merge(x1, o_a, o_b, o_c, gates, w_up_a[l], w_up_b[l], w_up_c[l], w_out[l], tm=tm)
        last = l == depth - 1
        gout = final_norm if last else ffn1_norm[l + 1]
        if last:
            xf = _ffn(x2, ffn2_norm[l], ffn2_w1[l], ffn2_w3[l], ffn2_w2[l], gout,
                      final=True, tm=tm, ff_chunk=ff_chunk)
        else:
            xf, _ = _ffn(x2, ffn2_norm[l], ffn2_w1[l], ffn2_w3[l], ffn2_w2[l], gout,
                         final=False, tm=tm, ff_chunk=ff_chunk)
    return xf.reshape(batch, seq, d)
```

```python
import functools
import math

import jax
import jax.numpy as jnp
from jax import lax
from jax.experimental import pallas as pl
from jax.experimental.pallas import tpu as pltpu

F32 = jnp.float32
BF16 = jnp.bfloat16

CHUNK = 64
DA_HEADS = 4
DA_DH = 64
DA_WIDTH = DA_HEADS * 2 * DA_DH
RW_HEADS = 4
RW_DH = 64
RW_WIDTH = RW_HEADS * RW_DH
RW_DECAY_RANK = 64
RW_A_RANK = 64
RW_GATE_RANK = 128
RW_COLS = 3 * RW_WIDTH + RW_DECAY_RANK + RW_A_RANK + RW_GATE_RANK
RW_CHUNK = 64
XA_HEADS = 4
XA_DH = 64
XA_WIDTH = XA_HEADS * XA_DH
ROPE_THETA = 500000.0
ROT_DIM = DA_DH // 4
ROT_HALF = ROT_DIM // 2
EPS = 1e-6
RW_GN_EPS = 64e-5
LANES = 128
NEG_BIG = -1e30
ATT_TK = 256
ATT_TILES_PER_STEP = 2
ATT_LOOKAHEAD = 4
LOG2E = math.log2(math.e)
VMEM_LIMIT = 56 * 1024 * 1024


def _cparams(*sem):
    return pltpu.CompilerParams(dimension_semantics=sem, vmem_limit_bytes=VMEM_LIMIT)


def _resident(shape):
    return pl.BlockSpec(shape, lambda *_: (0,) * len(shape), pipeline_mode=pl.Buffered(1))


def _mm(a, b):
    return jnp.dot(a.astype(BF16), b.astype(BF16), preferred_element_type=F32)


def _mm_nt(a, b):
    return lax.dot_general(a.astype(BF16), b.astype(BF16), (((1,), (1,)), ((), ())),
                           preferred_element_type=F32)


def _mm_tn(a, b):
    return lax.dot_general(a.astype(BF16), b.astype(BF16), (((0,), (0,)), ((), ())),
                           preferred_element_type=F32)


def _mm_f32(a, m_bf16):
    hi = a.astype(BF16)
    r1 = a - hi.astype(F32)
    mid = r1.astype(BF16)
    lo = (r1 - mid.astype(F32)).astype(BF16)
    dot = functools.partial(jnp.dot, preferred_element_type=F32)
    return dot(hi, m_bf16) + dot(mid, m_bf16) + dot(lo, m_bf16)


def _mm_f32_left(m_bf16, a):
    hi = a.astype(BF16)
    r1 = a - hi.astype(F32)
    mid = r1.astype(BF16)
    lo = (r1 - mid.astype(F32)).astype(BF16)
    dot = functools.partial(jnp.dot, preferred_element_type=F32)
    return dot(m_bf16, hi) + dot(m_bf16, mid) + dot(m_bf16, lo)


def _rms(x, gain):
    return x * lax.rsqrt(jnp.mean(x * x, axis=-1, keepdims=True) + EPS) * gain


def _ffn_kernel(x_ref, gin_ref, w1_ref, w3_ref, w2_ref, gout_ref, *out_refs, ff_chunk, final):
    x = x_ref[...]
    h = _rms(x, gin_ref[...]).astype(BF16)
    d_ff = w1_ref.shape[1]
    acc = jnp.zeros(x.shape, F32)
    for c in range(d_ff // ff_chunk):
        sl = slice(c * ff_chunk, (c + 1) * ff_chunk)
        a = jnp.dot(h, w1_ref[:, sl], preferred_element_type=F32)
        b = jnp.dot(h, w3_ref[:, sl], preferred_element_type=F32)
        act = (a * jax.nn.sigmoid(a) * b).astype(BF16)
        acc = acc + jnp.dot(act, w2_ref[sl, :], preferred_element_type=F32)
    y = x + 0.5 * acc
    if final:
        out_refs[0][...] = _rms(y, gout_ref[...])
    else:
        out_refs[0][...] = y
        out_refs[1][...] = _rms(y, gout_ref[...]).astype(BF16)


def _ffn(x, gin, w1, w3, w2, gout, *, final, tm, ff_chunk):
    t, d = x.shape
    d_ff = w1.shape[1]
    row = pl.BlockSpec((tm, d), lambda i: (i, 0))
    if final:
        out_shape = jax.ShapeDtypeStruct((t, d), F32)
        out_specs = row
    else:
        out_shape = (jax.ShapeDtypeStruct((t, d), F32), jax.ShapeDtypeStruct((t, d), BF16))
        out_specs = (row, row)
    return pl.pallas_call(
        functools.partial(_ffn_kernel, ff_chunk=ff_chunk, final=final),
        grid=(t // tm,),
        in_specs=[row, _resident((1, d)), _resident((d, d_ff)), _resident((d, d_ff)),
                  _resident((d_ff, d)), _resident((1, d))],
        out_specs=out_specs,
        out_shape=out_shape,
        compiler_params=_cparams("parallel"),
        name="ffn_final" if final else "ffn_first",
    )(x, gin.reshape(1, d), w1.astype(BF16), w3.astype(BF16), w2.astype(BF16), gout.reshape(1, d))


def _rope_kernel(pos_ref, invf_ref, cos_ref, sina_ref, sinb_ref):
    pos = pos_ref[...].astype(F32)
    ang = invf_ref[...] * pos
    c = jnp.cos(ang)
    s = jnp.sin(ang)
    one = jnp.ones_like(c)
    zero = jnp.zeros_like(c)
    pad = DA_DH // ROT_HALF - 2

    def lanes(first, second, fill):
        head = [first, second] + [fill] * pad
        return jnp.concatenate(head * (LANES // DA_DH), axis=0).T

    cos_ref[...] = lanes(c, c, one)
    sina_ref[...] = lanes(-s, zero, zero)
    sinb_ref[...] = lanes(zero, s, zero)


def _rope_tables(positions, *, ts):
    t = positions.size
    pos = positions.reshape(t // ts, 1, ts)
    inv_freq = jnp.power(jnp.float32(ROPE_THETA),
                         -jnp.arange(ROT_HALF, dtype=F32) * (2.0 / ROT_DIM)).reshape(ROT_HALF, 1)
    out = jax.ShapeDtypeStruct((t, LANES), F32)
    spec = pl.BlockSpec((ts, LANES), lambda i: (i, 0))
    return pl.pallas_call(
        _rope_kernel,
        grid=(t // ts,),
        in_specs=[pl.BlockSpec((None, 1, ts), lambda i: (i, 0, 0)), _resident((ROT_HALF, 1))],
        out_specs=(spec, spec, spec),
        out_shape=(out, out, out),
        compiler_params=_cparams("parallel"),
        name="rope_tables",
    )(pos, inv_freq)


def _in_proj_kernel(h_ref, w_ref, cos_ref, sina_ref, sinb_ref, gb_ref,
                    q_ref, k_ref, v_ref, u_ref, qx_ref, g_ref, *, d_model):
    h = h_ref[...]
    cos_t = cos_ref[...]
    sin_a = sina_ref[...]
    sin_b = sinb_ref[...]

    def seg(lo, width):
        return jnp.dot(h, w_ref[:, lo:lo + width], preferred_element_type=F32)

    def rope(t):
        nxt = pltpu.roll(t, LANES - ROT_HALF, axis=1)
        prv = pltpu.roll(t, ROT_HALF, axis=1)
        return t * cos_t + nxt * sin_a + prv * sin_b

    c = 0
    tm = h.shape[0]
    for j in range(DA_HEADS):
        t = rope(seg(c + j * LANES, LANES)) * (DA_DH ** -0.5 * LOG2E)
        q_ref[j * LANES:(j + 1) * LANES, :] = t.T.astype(BF16)
    c += DA_WIDTH
    for j in range(DA_HEADS):
        t = seg(c + j * LANES, LANES)
        k_ref[:, j * LANES:(j + 1) * LANES] = rope(t).astype(BF16)
    c += DA_WIDTH
    for j in range(DA_HEADS):
        t = seg(c + j * LANES, LANES)
        for jj in range(tm // ATT_TK):
            v_ref[j, jj] = t[jj * ATT_TK:(jj + 1) * ATT_TK, :].T.astype(BF16)
    c += DA_WIDTH
    u_ref[...] = seg(c, RW_COLS)
    c += RW_COLS
    qx_ref[...] = (seg(c, XA_WIDTH) * (XA_DH ** -0.5)).astype(BF16)
    c += XA_WIDTH
    for j in range(3):
        gl = seg(c + j * d_model, d_model) + gb_ref[:, j * d_model:(j + 1) * d_model]
        g_ref[:, j * d_model:(j + 1) * d_model] = jax.nn.sigmoid(gl).astype(BF16)


def _in_proj(h, w_in, cos_t, sin_a, sin_b, gate_bias, *, batch, seq, tm):
    t, d = h.shape
    cols = w_in.shape[1]
    nst = seq // tm
    row = lambda w: pl.BlockSpec((tm, w), lambda i: (i, 0))
    sd = jax.ShapeDtypeStruct
    out_shape = (sd((batch * DA_WIDTH, seq), BF16), sd((t, DA_WIDTH), BF16),
                 sd((batch * DA_HEADS, seq // ATT_TK, LANES, ATT_TK), BF16),
                 sd((t, RW_COLS), F32), sd((t, XA_WIDTH), BF16), sd((t, 3 * d), BF16))
    out_specs = (pl.BlockSpec((DA_WIDTH, tm), lambda i: (i // nst, i % nst)),
                 row(DA_WIDTH),
                 pl.BlockSpec((DA_HEADS, tm // ATT_TK, LANES, ATT_TK),
                              lambda i: (i // nst, i % nst, 0, 0)),
                 row(RW_COLS), row(XA_WIDTH), row(3 * d))
    return pl.pallas_call(
        functools.partial(_in_proj_kernel, d_model=d),
        grid=(t // tm,),
        in_specs=[row(d), _resident((d, cols)), row(LANES), row(LANES), row(LANES),
                  _resident((1, 3 * d))],
        out_specs=out_specs,
        out_shape=out_shape,
        compiler_params=_cparams("parallel"),
        name="in_proj",
    )(h, w_in.astype(BF16), cos_t, sin_a, sin_b, gate_bias.reshape(1, 3 * d))


def _lambda_kernel(q1_ref, k1_ref, q2_ref, k2_ref, out_ref, *, lam_init):
    l1 = jnp.exp(jnp.sum(q1_ref[...] * k1_ref[...], axis=-1, keepdims=True))
    l2 = jnp.exp(jnp.sum(q2_ref[...] * k2_ref[...], axis=-1, keepdims=True))
    out_ref[...] = jnp.broadcast_to(l1 - l2 + lam_init, out_ref.shape)


def _lambda(q1, k1, q2, k2, lam_init):
    vec = lambda a: a.reshape(1, DA_DH)
    out = pl.pallas_call(
        functools.partial(_lambda_kernel, lam_init=lam_init),
        out_shape=jax.ShapeDtypeStruct((8, LANES), F32),
        name="da_lambda",
    )(vec(q1), vec(k1), vec(q2), vec(k2))
    return out[0, :1]


def _diff_attn_kernel(lam_ref, q_ref, k_ref, v_ref, gain_ref, o_ref,
                      qm_s, m_s, l_s, acc_s, *, tq, out_scale):
    i = pl.program_id(2)
    lam = lam_ref[0]
    tk = ATT_TK
    ns = tq // tk
    q = q_ref[...]
    first = lax.broadcasted_iota(jnp.int32, (LANES, 1), 0) < DA_DH
    zero = jnp.zeros_like(q)
    qm_s[0] = jnp.where(first, q, zero)
    qm_s[1] = jnp.where(first, zero, q)
    m_s[...] = jnp.full(m_s.shape, NEG_BIG, F32)
    l_s[...] = jnp.zeros(l_s.shape, F32)
    acc_s[...] = jnp.zeros(acc_s.shape, F32)
    key_c = lax.broadcasted_iota(jnp.int32, (tk, tk), 0) // CHUNK
    qry_c = lax.broadcasted_iota(jnp.int32, (tk, tk), 1) // CHUNK
    causal = key_c <= qry_c

    def scores(unit):
        j, c, mi, _ = unit
        k = k_ref[pl.ds(pl.multiple_of(j * tk, tk), tk), :]
        return jnp.dot(k, qm_s[mi, :, c * tk:(c + 1) * tk], preferred_element_type=F32)

    def absorb(unit, s):
        j, c, mi, masked = unit
        cs = slice(c * tk, (c + 1) * tk)
        if masked:
            s = jnp.where(causal, s, NEG_BIG)
        m_old = m_s[mi, :, cs]
        m_new = jnp.maximum(m_old, jnp.max(s, axis=0, keepdims=True))
        alpha = jnp.exp2(m_old - m_new)
        p = jnp.exp2(s - m_new)
        m_s[mi, :, cs] = m_new
        l_s[mi, :, cs] = alpha * l_s[mi, :, cs] + jnp.sum(p, axis=0, keepdims=True)
        acc_s[mi, :, cs] = alpha * acc_s[mi, :, cs] + jnp.dot(
            v_ref[j], p.astype(BF16), preferred_element_type=F32)

    def run(units):
        pending = []
        for idx in range(len(units) + ATT_LOOKAHEAD):
            if idx < len(units):
                pending.append(scores(units[idx]))
            if idx >= ATT_LOOKAHEAD:
                absorb(units[idx - ATT_LOOKAHEAD], pending[idx - ATT_LOOKAHEAD])

    def body(jj, carry):
        run([(jj * ATT_TILES_PER_STEP + t, c, mi, False)
             for t in range(ATT_TILES_PER_STEP) for c in range(ns) for mi in range(2)])
        return carry

    lax.fori_loop(0, i * ns // ATT_TILES_PER_STEP, body, 0)
    run([(i * ns + d, c, mi, c == d)
         for d in range(ns) for c in range(d, ns) for mi in range(2)])
    o = acc_s[0] / l_s[0] - lam * (acc_s[1] / l_s[1])
    o = o * lax.rsqrt(jnp.mean(o * o, axis=0, keepdims=True) + EPS) * gain_ref[...]
    o_ref[...] = (o * out_scale).T.astype(BF16)


def _diff_attn(lam, q_t, k, v_t, gain, *, batch, seq, tq, out_scale):
    nq = seq // tq
    nk = seq // ATT_TK
    assert (tq // ATT_TK) % ATT_TILES_PER_STEP == 0
    return pl.pallas_call(
        functools.partial(_diff_attn_kernel, tq=tq, out_scale=out_scale),
        grid=(batch, DA_HEADS, nq),
        scratch_shapes=[pltpu.VMEM((2, LANES, tq), BF16), pltpu.VMEM((2, 1, tq), F32),
                        pltpu.VMEM((2, 1, tq), F32), pltpu.VMEM((2, LANES, tq), F32)],
        in_specs=[pl.BlockSpec(memory_space=pltpu.SMEM),
                  pl.BlockSpec((LANES, tq), lambda b, h, i: (b * DA_HEADS + h, i)),
                  pl.BlockSpec((seq, LANES), lambda b, h, i: (b, h)),
                  pl.BlockSpec((None, nk, LANES, ATT_TK),
                               lambda b, h, i: (b * DA_HEADS + h, 0, 0, 0)),
                  _resident((LANES, 1))],
        out_specs=pl.BlockSpec((tq, LANES), lambda b, h, i: (b * nq + i, h)),
        out_shape=jax.ShapeDtypeStruct(k.shape, BF16),
        compiler_params=_cparams("parallel", "parallel", "arbitrary"),
        name="diff_attn",
    )(lam, q_t, k, v_t, gain.reshape(LANES, 1))


def _head_mask(h, width=RW_WIDTH):
    lane = lax.broadcasted_iota(jnp.int32, (1, width), 1)
    return (lane // RW_DH) == h


def _stack_heads(x):
    zero = jnp.zeros_like(x)
    return jnp.concatenate([jnp.where(_head_mask(h), x, zero) for h in range(RW_HEADS)], axis=0)


def _rwkv_kernel(u_ref, mu_ref, wlr_ref, wab_ref, g2_ref, kk_ref, ka_ref, rk_ref,
                 lnw_ref, lnb_ref, o_ref,
                 carry_ref, state_ref, r_s, k_s, v_s, lw_s, cum_s, kn_s, b_s, y_s, *, ts):
    c_len = RW_CHUNK
    w = RW_WIDTH

    @pl.when(pl.program_id(1) == 0)
    def _():
        carry_ref[...] = jnp.zeros_like(carry_ref)
        state_ref[...] = jnp.zeros_like(state_ref)

    u = u_ref[...]
    row = lax.broadcasted_iota(jnp.int32, (ts, 1), 0)
    prev = jnp.where(row == 0, carry_ref[...], pltpu.roll(u, 1, axis=0))
    carry_ref[...] = u[ts - 1:ts, :]
    us = u + (prev - u) * mu_ref[...]
    r = us[:, 0:w]
    k = us[:, w:2 * w]
    v = us[:, 2 * w:3 * w]
    low = us[:, 3 * w:3 * w + LANES]
    lane = lax.broadcasted_iota(jnp.int32, (1, LANES), 1)
    low = jnp.where(lane < RW_DECAY_RANK, jnp.tanh(low), low)
    lr = jnp.dot(low, wlr_ref[...], preferred_element_type=F32,
                 precision=lax.Precision.HIGHEST)
    z = -(wab_ref[:, 0:w] + lr[:, 0:w])
    softplus = jnp.maximum(z, 0.0) + jnp.log(1.0 + jnp.exp(-jnp.abs(z)))
    lw = -jnp.exp(-softplus - 0.5)
    a = jax.nn.sigmoid(wab_ref[:, w:2 * w] + lr[:, w:2 * w])
    g = _mm(jax.nn.sigmoid(us[:, 3 * w + LANES:3 * w + LANES + RW_GATE_RANK]), g2_ref[...])

    hr = lax.broadcasted_iota(jnp.int32, (w, w), 0) // RW_DH
    hc = lax.broadcasted_iota(jnp.int32, (w, w), 1) // RW_DH
    head_ones = (hr == hc).astype(BF16)
    kk = k * kk_ref[...]
    kk = kk / jnp.maximum(jnp.sqrt(_mm_f32(kk * kk, head_ones)), 1e-12)
    k2 = k * (1.0 + (a - 1.0) * ka_ref[...])

    tr = lax.broadcasted_iota(jnp.int32, (ts, ts), 0)
    tc = lax.broadcasted_iota(jnp.int32, (ts, ts), 1)
    tri = ((tc <= tr) & (tc // c_len == tr // c_len)).astype(BF16)
    cum = _mm_f32_left(tri, lw)

    r_s[...] = r
    k_s[...] = k2
    v_s[...] = v
    lw_s[...] = lw
    cum_s[...] = cum
    kn_s[...] = kk
    b_s[...] = kk * a

    n4 = RW_HEADS * c_len
    tt = lax.broadcasted_iota(jnp.int32, (c_len, n4), 0)
    ss = lax.broadcasted_iota(jnp.int32, (c_len, n4), 1) % c_len
    strict = ss < tt
    incl = ss <= tt
    br = lax.broadcasted_iota(jnp.int32, (n4, n4), 0)
    bc = lax.broadcasted_iota(jnp.int32, (n4, n4), 1)
    blockdiag = (br // c_len) == (bc // c_len)
    eye = (br == bc).astype(F32)

    def rowblock_sum(x):
        return sum(x[h * c_len:(h + 1) * c_len, :] for h in range(RW_HEADS))

    def chunk(c, state):
        sl = pl.ds(pl.multiple_of(c * c_len, c_len), c_len)
        lw_c = lw_s[sl, :]
        cum_c = cum_s[sl, :]
        last = cum_c[c_len - 1:c_len, :]
        kn = kn_s[sl, :]
        bv = b_s[sl, :]
        k2_c = k_s[sl, :]
        v_c = v_s[sl, :]
        inv = jnp.exp(-cum_c)
        tail = jnp.exp(last - cum_c)
        a_t = -kn * jnp.exp(cum_c - lw_c)
        r_t = r_s[sl, :] * jnp.exp(cum_c)
        ar = jnp.concatenate([a_t, r_t], axis=0)
        bk = jnp.concatenate([_stack_heads(bv * inv), _stack_heads(k2_c * inv)], axis=0)
        gram = _mm_nt(ar, bk)
        zero = jnp.zeros((c_len, n4), F32)
        a_ab = jnp.where(strict, gram[:c_len, :n4], zero)
        a_ak = jnp.where(strict, gram[:c_len, n4:], zero)
        a_rb = jnp.where(incl, gram[c_len:, :n4], zero)
        a_rk = jnp.where(incl, gram[c_len:, n4:], zero)

        x = jnp.where(blockdiag, jnp.concatenate([a_ab] * RW_HEADS, axis=0), 0.0)
        t_inv = eye + x
        steps = int(math.log2(c_len)) - 1
        for _ in range(steps):
            x = _mm(x, x)
            t_inv = t_inv + _mm(t_inv, x)
        t_cat = rowblock_sum(t_inv)

        v_bd = _stack_heads(v_c)
        av = _mm(jnp.concatenate([a_ak, a_rk], axis=0), v_bd)
        xy = _mm_nt(ar, state)
        z_c = xy[:c_len, :] + av[:c_len, :]
        u_c = _mm(t_cat, _stack_heads(z_c))
        y_c = xy[c_len:, :] + av[c_len:, :] + _mm(a_rb, _stack_heads(u_c))
        y_s[sl, :] = y_c
        upd = _mm_tn(jnp.concatenate([u_c, v_c], axis=0),
                     jnp.concatenate([bv * tail, k2_c * tail], axis=0))
        return state * jnp.exp(last) + jnp.where(blockdiag, upd, 0.0)

    state_ref[...] = lax.fori_loop(0, ts // c_len, chunk, state_ref[...])

    y = y_s[...]
    mean = _mm_f32(y, head_ones) * (1.0 / RW_DH)
    yc = y - mean
    var = _mm_f32(yc * yc, head_ones) * (1.0 / RW_DH)
    yn = yc * lax.rsqrt(var + RW_GN_EPS) * lnw_ref[...] + lnb_ref[...]
    bonus = _mm_f32(r * k2 * rk_ref[...], head_ones) * v
    o_ref[...] = ((yn + bonus) * g).astype(BF16)


def _rwkv(u, mu, w0, w2, a0, a2, g2, k_k, k_a, r_k, ln_w, ln_b, *, batch, seq, ts):
    w = RW_WIDTH
    ns = seq // ts
    zeros = lambda r, c: jnp.zeros((r, c), F32)
    w_lr = jnp.concatenate([jnp.concatenate([w2, zeros(RW_DECAY_RANK, w)], axis=1),
                            jnp.concatenate([zeros(RW_A_RANK, w), a2], axis=1)], axis=0)
    w_ab = jnp.concatenate([w0, a0]).reshape(1, 2 * w)
    vec = lambda p: p.reshape(1, w)
    scratch = [pltpu.VMEM((1, RW_COLS), F32), pltpu.VMEM((w, w), F32)]
    scratch += [pltpu.VMEM((ts, w), F32) for _ in range(8)]
    return pl.pallas_call(
        functools.partial(_rwkv_kernel, ts=ts),
        grid=(batch, ns),
        in_specs=[pl.BlockSpec((ts, RW_COLS), lambda b, i: (b * ns + i, 0)),
                  _resident((1, RW_COLS)), _resident((LANES, 2 * w)), _resident((1, 2 * w)),
                  _resident((RW_GATE_RANK, w))] + [_resident((1, w))] * 5,
        out_specs=pl.BlockSpec((ts, w), lambda b, i: (b * ns + i, 0)),
        out_shape=jax.ShapeDtypeStruct((batch * seq, w), BF16),
        scratch_shapes=scratch,
        compiler_params=_cparams("parallel", "arbitrary"),
        name="rwkv7",
    )(u, mu.reshape(1, RW_COLS), w_lr, w_ab, g2.astype(BF16), vec(k_k), vec(k_a), vec(r_k),
      vec(ln_w), vec(ln_b))


def _mem_kv_kernel(mem_ref, g_ref, w_ref, kv_ref):
    h = _rms(mem_ref[...], g_ref[...]).astype(BF16)
    kv_ref[...] = jnp.dot(h, w_ref[...], preferred_element_type=F32).astype(BF16)


def _mem_kv(mem, gain, w_kv, *, batch, mem_len):
    d = mem.shape[-1]
    n = w_kv.shape[1]
    return pl.pallas_call(
        _mem_kv_kernel,
        grid=(batch,),
        in_specs=[pl.BlockSpec((mem_len, d), lambda b: (b, 0)), _resident((1, d)),
                  _resident((d, n))],
        out_specs=pl.BlockSpec((mem_len, n), lambda b: (b, 0)),
        out_shape=jax.ShapeDtypeStruct((batch * mem_len, n), BF16),
        compiler_params=_cparams("parallel"),
        name="mem_kv",
    )(mem.reshape(batch * mem_len, d), gain.reshape(1, d), w_kv.astype(BF16))


def _mem_attn_kernel(q_ref, kv_ref, o_ref):
    q = q_ref[...]
    k = kv_ref[:, 0:XA_WIDTH]
    v = kv_ref[:, XA_WIDTH:2 * XA_WIDTH]
    zero = jnp.zeros_like(q)
    out = jnp.zeros(q.shape, F32)
    for h in range(XA_HEADS):
        hm = _head_mask(h, XA_WIDTH)
        s = _mm_nt(jnp.where(hm, q, zero), k)
        p = jnp.exp(s - jnp.max(s, axis=-1, keepdims=True))
        o = jnp.dot(p.astype(BF16), v, preferred_element_type=F32)
        out = jnp.where(hm, o / jnp.sum(p, axis=-1, keepdims=True), out)
    o_ref[...] = out.astype(BF16)


def _mem_attn(qx, kv, *, batch, seq, mem_len, tq):
    nq = seq // tq
    spec = pl.BlockSpec((tq, XA_WIDTH), lambda b, i: (b * nq + i, 0))
    return pl.pallas_call(
        _mem_attn_kernel,
        grid=(batch, nq),
        in_specs=[spec, pl.BlockSpec((mem_len, 2 * XA_WIDTH), lambda b, i: (b, 0))],
        out_specs=spec,
        out_shape=jax.ShapeDtypeStruct(qx.shape, BF16),
        compiler_params=_cparams("parallel", "arbitrary"),
        name="mem_attn",
    )(qx, kv)


def _merge_kernel(x_ref, oa_ref, ob_ref, oc_ref, g_ref, wa_ref, wb_ref, wc_ref, wo_ref, o_ref):
    d = x_ref.shape[1]
    dot = functools.partial(jnp.dot, preferred_element_type=F32)
    merged = (g_ref[:, 0:d].astype(F32) * dot(oa_ref[...], wa_ref[...])
              + g_ref[:, d:2 * d].astype(F32) * dot(ob_ref[...], wb_ref[...])
              + g_ref[:, 2 * d:3 * d].astype(F32) * dot(oc_ref[...], wc_ref[...]))
    o_ref[...] = x_ref[...] + dot(merged.astype(BF16), wo_ref[...])


def _merge(x, oa, ob, oc, gates, wa, wb, wc, wo, *, tm):
    t, d = x.shape
    row = lambda w: pl.BlockSpec((tm, w), lambda i: (i, 0))
    return pl.pallas_call(
        _merge_kernel,
        grid=(t // tm,),
        in_specs=[row(d), row(DA_WIDTH), row(RW_WIDTH), row(XA_WIDTH), row(3 * d),
                  _resident(wa.shape), _resident(wb.shape), _resident(wc.shape),
                  _resident(wo.shape)],
        out_specs=row(d),
        out_shape=jax.ShapeDtypeStruct((t, d), F32),
        compiler_params=_cparams("parallel"),
        name="merge",
    )(x, oa, ob, oc, gates, wa.astype(BF16), wb.astype(BF16), wc.astype(BF16), wo.astype(BF16))


def _pick(n, pref):
    t = min(n, pref)
    while n % t:
        t //= 2
    return t


def kernel(x, mem, positions, ffn1_norm, ffn1_w1, ffn1_w3, ffn1_w2, mix_norm, w_in, gate_bias, da_lambda_q1, da_lambda_k1, da_lambda_q2, da_lambda_k2, da_subln, rw_mu, rw_w0, rw_w2, rw_a0, rw_a2, rw_g2, rw_k_k, rw_k_a, rw_r_k, rw_ln_w, rw_ln_b, mem_norm, w_mem_kv, w_up_a, w_up_b, w_up_c, w_out, ffn2_norm, ffn2_w1, ffn2_w3, ffn2_w2, final_norm):
    batch, seq, d = x.shape
    mem_len = mem.shape[1]
    depth = ffn1_norm.shape[0]
    t = batch * seq
    tm = _pick(seq, 512)
    d_ff = ffn1_w1.shape[-1]
    ff_chunk = d_ff // 4 if (d_ff // 4) % LANES == 0 else d_ff
    tq = _pick(seq, 1024)
    ts = _pick(seq, 512)

    cos_t, sin_a, sin_b = _rope_tables(positions, ts=_pick(t, 1024))
    xf = x.reshape(t, d)
    for l in range(depth):
        lam_init = 0.8 - 0.6 * math.exp(-0.3 * l)
        norm_after = mix_norm[l]
        x1, h = _ffn(xf, ffn1_norm[l], ffn1_w1[l], ffn1_w3[l], ffn1_w2[l], norm_after,
                     final=False, tm=tm, ff_chunk=ff_chunk)
        q, k, v, u, qx, gates = _in_proj(h, w_in[l], cos_t, sin_a, sin_b, gate_bias[l],
                                         batch=batch, seq=seq, tm=tm)
        lam = _lambda(da_lambda_q1[l], da_lambda_k1[l], da_lambda_q2[l], da_lambda_k2[l], lam_init)
        o_a = _diff_attn(lam, q, k, v, da_subln[l], batch=batch, seq=seq, tq=tq,
                         out_scale=1.0 - lam_init)
        o_b = _rwkv(u, rw_mu[l], rw_w0[l], rw_w2[l], rw_a0[l], rw_a2[l], rw_g2[l], rw_k_k[l],
                    rw_k_a[l], rw_r_k[l], rw_ln_w[l], rw_ln_b[l], batch=batch, seq=seq, ts=ts)
        kv = _mem_kv(mem, mem_norm[l], w_mem_kv[l], batch=batch, mem_len=mem_len)
        o_c = _mem_attn(qx, kv, batch=batch, seq=seq, mem_len=mem_len, tq=_pick(seq, 512))
        x2 = _merge(x1, o_a, o_b, o_c, gates, w_up_a[l], w_up_b[l], w_up_c[l], w_out[l], tm=tm)
        last = l == depth - 1
        gout = final_norm if last else ffn1_norm[l + 1]
        if last:
            xf = _ffn(x2, ffn2_norm[l], ffn2_w1[l], ffn2_w3[l], ffn2_w2[l], gout,
                      final=True, tm=tm, ff_chunk=ff_chunk)
        else:
            xf, _ = _ffn(x2, ffn2_norm[l], ffn2_w1[l], ffn2_w3[l], ffn2_w2[l], gout,
                         final=False, tm=tm, ff_chunk=ff_chunk)
    return xf.reshape(batch, seq, d)
```

```python
import functools
import math

import jax
import jax.numpy as jnp
from jax import lax
from jax.experimental import pallas as pl
from jax.experimental.pallas import tpu as pltpu

F32 = jnp.float32
BF16 = jnp.bfloat16

CHUNK = 64
DA_HEADS = 4
DA_DH = 64
DA_WIDTH = DA_HEADS * 2 * DA_DH
RW_HEADS = 4
RW_DH = 64
RW_WIDTH = RW_HEADS * RW_DH
RW_DECAY_RANK = 64
RW_A_RANK = 64
RW_GATE_RANK = 128
RW_COLS = 3 * RW_WIDTH + RW_DECAY_RANK + RW_A_RANK + RW_GATE_RANK
RW_CHUNK = 64
RW_CUM_PASSES = 2
RW_SEQS = 2
RW_GROUP = 2
XA_HEADS = 4
XA_DH = 64
XA_WIDTH = XA_HEADS * XA_DH
ROPE_THETA = 500000.0
ROT_DIM = DA_DH // 4
ROT_HALF = ROT_DIM // 2
EPS = 1e-6
RW_GN_EPS = 64e-5
LANES = 128
NEG_BIG = -1e30
ATT_TK = 256
ATT_TILES_PER_STEP = 4
ATT_LOOKAHEAD = 4
LOG2E = math.log2(math.e)
VMEM_LIMIT = 56 * 1024 * 1024


def _cparams(*sem):
    return pltpu.CompilerParams(dimension_semantics=sem, vmem_limit_bytes=VMEM_LIMIT)


def _resident(shape):
    return pl.BlockSpec(shape, lambda *_: (0,) * len(shape), pipeline_mode=pl.Buffered(1))


def _mm(a, b):
    return jnp.dot(a.astype(BF16), b.astype(BF16), preferred_element_type=F32)


def _mm_nt(a, b):
    return lax.dot_general(a.astype(BF16), b.astype(BF16), (((1,), (1,)), ((), ())),
                           preferred_element_type=F32)


def _split_bf16(a, passes):
    parts = []
    for _ in range(passes):
        part = a.astype(BF16)
        parts.append(part)
        a = a - part.astype(F32)
    return parts


def _mm_f32_left(m_bf16, a, passes):
    return sum(jnp.dot(m_bf16, p, preferred_element_type=F32) for p in _split_bf16(a, passes))


def _rms(x, gain):
    return x * lax.rsqrt(jnp.mean(x * x, axis=-1, keepdims=True) + EPS) * gain


def _ffn_kernel(x_ref, gin_ref, w1_ref, w3_ref, w2_ref, gout_ref, *out_refs, ff_chunk, final):
    x = x_ref[...]
    h = _rms(x, gin_ref[...]).astype(BF16)
    d_ff = w1_ref.shape[1]
    acc = jnp.zeros(x.shape, F32)
    for c in range(d_ff // ff_chunk):
        sl = slice(c * ff_chunk, (c + 1) * ff_chunk)
        a = jnp.dot(h, w1_ref[:, sl], preferred_element_type=F32)
        b = jnp.dot(h, w3_ref[:, sl], preferred_element_type=F32)
        act = (a * jax.nn.sigmoid(a) * b).astype(BF16)
        acc = acc + jnp.dot(act, w2_ref[sl, :], preferred_element_type=F32)
    y = x + 0.5 * acc
    if final:
        out_refs[0][...] = _rms(y, gout_ref[...])
    else:
        out_refs[0][...] = y
        out_refs[1][...] = _rms(y, gout_ref[...]).astype(BF16)


def _ffn(x, gin, w1, w3, w2, gout, *, final, tm, ff_chunk):
    t, d = x.shape
    d_ff = w1.shape[1]
    row = pl.BlockSpec((tm, d), lambda i: (i, 0))
    if final:
        out_shape = jax.ShapeDtypeStruct((t, d), F32)
        out_specs = row
    else:
        out_shape = (jax.ShapeDtypeStruct((t, d), F32), jax.ShapeDtypeStruct((t, d), BF16))
        out_specs = (row, row)
    return pl.pallas_call(
        functools.partial(_ffn_kernel, ff_chunk=ff_chunk, final=final),
        grid=(t // tm,),
        in_specs=[row, _resident((1, d)), _resident((d, d_ff)), _resident((d, d_ff)),
                  _resident((d_ff, d)), _resident((1, d))],
        out_specs=out_specs,
        out_shape=out_shape,
        compiler_params=_cparams("parallel"),
        name="ffn_final" if final else "ffn_first",
    )(x, gin.reshape(1, d), w1.astype(BF16), w3.astype(BF16), w2.astype(BF16), gout.reshape(1, d))


def _rope_kernel(pos_ref, invf_ref, cos_ref, sina_ref, sinb_ref):
    pos = pos_ref[...].astype(F32)
    ang = invf_ref[...] * pos
    c = jnp.cos(ang)
    s = jnp.sin(ang)
    one = jnp.ones_like(c)
    zero = jnp.zeros_like(c)
    pad = DA_DH // ROT_HALF - 2

    def lanes(first, second, fill):
        head = [first, second] + [fill] * pad
        return jnp.concatenate(head * (LANES // DA_DH), axis=0).T

    cos_ref[...] = lanes(c, c, one)
    sina_ref[...] = lanes(-s, zero, zero)
    sinb_ref[...] = lanes(zero, s, zero)


def _rope_tables(positions, *, ts):
    t = positions.size
    pos = positions.reshape(t // ts, 1, ts)
    inv_freq = jnp.power(jnp.float32(ROPE_THETA),
                         -jnp.arange(ROT_HALF, dtype=F32) * (2.0 / ROT_DIM)).reshape(ROT_HALF, 1)
    out = jax.ShapeDtypeStruct((t, LANES), F32)
    spec = pl.BlockSpec((ts, LANES), lambda i: (i, 0))
    return pl.pallas_call(
        _rope_kernel,
        grid=(t // ts,),
        in_specs=[pl.BlockSpec((None, 1, ts), lambda i: (i, 0, 0)), _resident((ROT_HALF, 1))],
        out_specs=(spec, spec, spec),
        out_shape=(out, out, out),
        compiler_params=_cparams("parallel"),
        name="rope_tables",
    )(pos, inv_freq)


def _in_proj_kernel(h_ref, w_ref, cos_ref, sina_ref, sinb_ref, gb_ref,
                    q_ref, k_ref, v_ref, u_ref, qx_ref, g_ref, *, d_model):
    h = h_ref[...]
    cos_t = cos_ref[...]
    sin_a = sina_ref[...]
    sin_b = sinb_ref[...]

    def seg(lo, width):
        return jnp.dot(h, w_ref[:, lo:lo + width], preferred_element_type=F32)

    def rope(t):
        nxt = pltpu.roll(t, LANES - ROT_HALF, axis=1)
        prv = pltpu.roll(t, ROT_HALF, axis=1)
        return t * cos_t + nxt * sin_a + prv * sin_b

    c = 0
    tm = h.shape[0]
    for j in range(DA_HEADS):
        t = rope(seg(c + j * LANES, LANES)) * (DA_DH ** -0.5 * LOG2E)
        q_ref[j * LANES:(j + 1) * LANES, :] = t.T.astype(BF16)
    c += DA_WIDTH
    for j in range(DA_HEADS):
        t = seg(c + j * LANES, LANES)
        k_ref[:, j * LANES:(j + 1) * LANES] = rope(t).astype(BF16)
    c += DA_WIDTH
    for j in range(DA_HEADS):
        t = seg(c + j * LANES, LANES)
        for jj in range(tm // ATT_TK):
            v_ref[j, jj] = t[jj * ATT_TK:(jj + 1) * ATT_TK, :].T.astype(BF16)
    c += DA_WIDTH
    u_ref[...] = seg(c, RW_COLS)
    c += RW_COLS
    qx_ref[...] = (seg(c, XA_WIDTH) * (XA_DH ** -0.5)).astype(BF16)
    c += XA_WIDTH
    for j in range(3):
        gl = seg(c + j * d_model, d_model) + gb_ref[:, j * d_model:(j + 1) * d_model]
        g_ref[:, j * d_model:(j + 1) * d_model] = jax.nn.sigmoid(gl).astype(BF16)


def _in_proj(h, w_in, cos_t, sin_a, sin_b, gate_bias, *, batch, seq, tm):
    t, d = h.shape
    cols = w_in.shape[1]
    nst = seq // tm
    row = lambda w: pl.BlockSpec((tm, w), lambda i: (i, 0))
    sd = jax.ShapeDtypeStruct
    out_shape = (sd((batch * DA_WIDTH, seq), BF16), sd((t, DA_WIDTH), BF16),
                 sd((batch * DA_HEADS, seq // ATT_TK, LANES, ATT_TK), BF16),
                 sd((t, RW_COLS), F32), sd((t, XA_WIDTH), BF16), sd((t, 3 * d), BF16))
    out_specs = (pl.BlockSpec((DA_WIDTH, tm), lambda i: (i // nst, i % nst)),
                 row(DA_WIDTH),
                 pl.BlockSpec((DA_HEADS, tm // ATT_TK, LANES, ATT_TK),
                              lambda i: (i // nst, i % nst, 0, 0)),
                 row(RW_COLS), row(XA_WIDTH), row(3 * d))
    return pl.pallas_call(
        functools.partial(_in_proj_kernel, d_model=d),
        grid=(t // tm,),
        in_specs=[row(d), _resident((d, cols)), row(LANES), row(LANES), row(LANES),
                  _resident((1, 3 * d))],
        out_specs=out_specs,
        out_shape=out_shape,
        compiler_params=_cparams("parallel"),
        name="in_proj",
    )(h, w_in.astype(BF16), cos_t, sin_a, sin_b, gate_bias.reshape(1, 3 * d))


def _lambda_kernel(q1_ref, k1_ref, q2_ref, k2_ref, out_ref, *, lam_init):
    l1 = jnp.exp(jnp.sum(q1_ref[...] * k1_ref[...], axis=-1, keepdims=True))
    l2 = jnp.exp(jnp.sum(q2_ref[...] * k2_ref[...], axis=-1, keepdims=True))
    out_ref[...] = jnp.broadcast_to(l1 - l2 + lam_init, out_ref.shape)


def _lambda(q1, k1, q2, k2, lam_init):
    vec = lambda a: a.reshape(1, DA_DH)
    out = pl.pallas_call(
        functools.partial(_lambda_kernel, lam_init=lam_init),
        out_shape=jax.ShapeDtypeStruct((8, LANES), F32),
        name="da_lambda",
    )(vec(q1), vec(k1), vec(q2), vec(k2))
    return out[0, :1]


def _diff_attn_kernel(lam_ref, q_ref, k_ref, v_ref, gain_ref, o_ref,
                      qm_s, m_s, l_s, acc_s, *, tq, out_scale):
    i = pl.program_id(2)
    lam = lam_ref[0]
    tk = ATT_TK
    ns = tq // tk
    q = q_ref[...]
    first = lax.broadcasted_iota(jnp.int32, (LANES, 1), 0) < DA_DH
    zero = jnp.zeros_like(q)
    qm_s[0] = jnp.where(first, q, zero)
    qm_s[1] = jnp.where(first, zero, q)
    m_s[...] = jnp.full(m_s.shape, NEG_BIG, F32)
    l_s[...] = jnp.zeros(l_s.shape, F32)
    acc_s[...] = jnp.zeros(acc_s.shape, F32)
    key_c = lax.broadcasted_iota(jnp.int32, (tk, tk), 0) // CHUNK
    qry_c = lax.broadcasted_iota(jnp.int32, (tk, tk), 1) // CHUNK
    causal = key_c <= qry_c

    def scores(unit):
        j, c, mi, _ = unit
        k = k_ref[pl.ds(pl.multiple_of(j * tk, tk), tk), :]
        return jnp.dot(k, qm_s[mi, :, c * tk:(c + 1) * tk], preferred_element_type=F32)

    def absorb(unit, s):
        j, c, mi, masked = unit
        cs = slice(c * tk, (c + 1) * tk)
        if masked:
            s = jnp.where(causal, s, NEG_BIG)
        m_old = m_s[mi, :, cs]
        m_new = jnp.maximum(m_old, jnp.max(s, axis=0, keepdims=True))
        alpha = jnp.exp2(m_old - m_new)
        p = jnp.exp2(s - m_new)
        m_s[mi, :, cs] = m_new
        l_s[mi, :, cs] = alpha * l_s[mi, :, cs] + jnp.sum(p, axis=0, keepdims=True)
        acc_s[mi, :, cs] = alpha * acc_s[mi, :, cs] + jnp.dot(
            v_ref[j], p.astype(BF16), preferred_element_type=F32)

    def run(units):
        pending = []
        for idx in range(len(units) + ATT_LOOKAHEAD):
            if idx < len(units):
                pending.append(scores(units[idx]))
            if idx >= ATT_LOOKAHEAD:
                absorb(units[idx - ATT_LOOKAHEAD], pending[idx - ATT_LOOKAHEAD])

    def body(jj, carry):
        run([(jj * ATT_TILES_PER_STEP + t, c, mi, False)
             for t in range(ATT_TILES_PER_STEP) for c in range(ns) for mi in range(2)])
        return carry

    lax.fori_loop(0, i * ns // ATT_TILES_PER_STEP, body, 0)
    run([(i * ns + d, c, mi, c == d)
         for d in range(ns) for c in range(d, ns) for mi in range(2)])
    o = acc_s[0] / l_s[0] - lam * (acc_s[1] / l_s[1])
    o = o * lax.rsqrt(jnp.mean(o * o, axis=0, keepdims=True) + EPS) * gain_ref[...]
    o_ref[...] = (o * out_scale).T.astype(BF16)


def _diff_attn(lam, q_t, k, v_t, gain, *, batch, seq, tq, out_scale):
    nq = seq // tq
    nk = seq // ATT_TK
    assert (tq // ATT_TK) % ATT_TILES_PER_STEP == 0
    return pl.pallas_call(
        functools.partial(_diff_attn_kernel, tq=tq, out_scale=out_scale),
        grid=(batch, DA_HEADS, nq),
        scratch_shapes=[pltpu.VMEM((2, LANES, tq), BF16), pltpu.VMEM((2, 1, tq), F32),
                        pltpu.VMEM((2, 1, tq), F32), pltpu.VMEM((2, LANES, tq), F32)],
        in_specs=[pl.BlockSpec(memory_space=pltpu.SMEM),
                  pl.BlockSpec((LANES, tq), lambda b, h, i: (b * DA_HEADS + h, i)),
                  pl.BlockSpec((seq, LANES), lambda b, h, i: (b, h)),
                  pl.BlockSpec((None, nk, LANES, ATT_TK),
                               lambda b, h, i: (b * DA_HEADS + h, 0, 0, 0)),
                  _resident((LANES, 1))],
        out_specs=pl.BlockSpec((tq, LANES), lambda b, h, i: (b * nq + i, h)),
        out_shape=jax.ShapeDtypeStruct(k.shape, BF16),
        compiler_params=_cparams("parallel", "parallel", "arbitrary"),
        name="diff_attn",
    )(lam, q_t, k, v_t, gain.reshape(LANES, 1))


def _head_mask(h, width=RW_WIDTH):
    lane = lax.broadcasted_iota(jnp.int32, (1, width), 1)
    return (lane // RW_DH) == h


def _stack_heads(x):
    zero = jnp.zeros_like(x)
    return jnp.concatenate([jnp.where(_head_mask(h), x, zero) for h in range(RW_HEADS)], axis=0)


def _rwkv_kernel(u_ref, mu_ref, wlr_ref, wab_ref, g2_ref, kk_ref, ka_ref, rk_ref,
                 lnw_ref, lnb_ref, o_ref,
                 carry_ref, state_ref, y_s, ar_s, arb_s, tcat_s, av_s, bkh_s, dec_s, *, ts):
    c_len = RW_CHUNK
    w = RW_WIDTH
    n4 = RW_HEADS * c_len
    n_chunks = ts // c_len
    groups = [range(g0, min(g0 + RW_GROUP, n_chunks)) for g0 in range(0, n_chunks, RW_GROUP)]

    @pl.when(pl.program_id(1) == 0)
    def _():
        carry_ref[...] = jnp.zeros_like(carry_ref)
        state_ref[...] = jnp.zeros_like(state_ref)

    hr = lax.broadcasted_iota(jnp.int32, (w, w), 0) // RW_DH
    hc = lax.broadcasted_iota(jnp.int32, (w, w), 1) // RW_DH
    head_ones = (hr == hc).astype(BF16)
    tr = lax.broadcasted_iota(jnp.int32, (c_len, c_len), 0)
    tc = lax.broadcasted_iota(jnp.int32, (c_len, c_len), 1)
    tri = (tc <= tr).astype(BF16)
    tt = lax.broadcasted_iota(jnp.int32, (c_len, n4), 0)
    ss = lax.broadcasted_iota(jnp.int32, (c_len, n4), 1) % c_len
    strict = ss < tt
    incl = ss <= tt
    br = lax.broadcasted_iota(jnp.int32, (n4, n4), 0)
    bc = lax.broadcasted_iota(jnp.int32, (n4, n4), 1)
    blockdiag = (br // c_len) == (bc // c_len)
    eye = (br == bc).astype(F32)
    levels = int(math.log2(c_len)) - 1
    lane = lax.broadcasted_iota(jnp.int32, (1, LANES), 1)

    def rowblock_sum(x):
        return sum(x[h * c_len:(h + 1) * c_len, :] for h in range(RW_HEADS))

    tok = {}
    xs = {}

    def shift_and_heads(q, group):
        lo, hi = group[0] * c_len, (group[-1] + 1) * c_len
        n = hi - lo
        u = u_ref[q, lo:hi, :]
        before = carry_ref[q] if lo == 0 else u_ref[q, lo - 1:lo, :]
        row = lax.broadcasted_iota(jnp.int32, (n, 1), 0)
        prev = jnp.where(row == 0, before, pltpu.roll(u, 1, axis=0))
        us = u + (prev - u) * mu_ref[...]
        r = us[:, 0:w]
        k = us[:, w:2 * w]
        v = us[:, 2 * w:3 * w]
        low = us[:, 3 * w:3 * w + LANES]
        low = jnp.where(lane < RW_DECAY_RANK, jnp.tanh(low), low)
        lr = _mm(low, wlr_ref[...])
        g = _mm(jax.nn.sigmoid(us[:, 3 * w + LANES:3 * w + LANES + RW_GATE_RANK]), g2_ref[...])
        kk = k * kk_ref[...]
        ksq = _mm(kk * kk, head_ones)
        yield
        z = -(wab_ref[:, 0:w] + lr[:, 0:w])
        softplus = jnp.maximum(z, 0.0) + jnp.log(1.0 + jnp.exp(-jnp.abs(z)))
        lw = -jnp.exp(-softplus - 0.5)
        a = jax.nn.sigmoid(wab_ref[:, w:2 * w] + lr[:, w:2 * w])
        kk = kk / jnp.maximum(jnp.sqrt(ksq), 1e-12)
        k2 = k * (1.0 + (a - 1.0) * ka_ref[...])
        for i, c in enumerate(group):
            sl = slice(i * c_len, (i + 1) * c_len)
            tok[q, c] = dict(r=r[sl], k2=k2[sl], v=v[sl], lw=lw[sl], kk=kk[sl],
                             bv=kk[sl] * a[sl], g=g[sl])
        yield

    def front(q, c):
        t = tok[q, c]
        cum_c = _mm_f32_left(tri, t["lw"], RW_CUM_PASSES)
        yield
        last = cum_c[c_len - 1:c_len, :]
        inv = jnp.exp(-cum_c)
        tail = jnp.exp(last - cum_c)
        a_t = -t["kk"] * jnp.exp(cum_c - t["lw"])
        r_t = t["r"] * jnp.exp(cum_c)
        ar = jnp.concatenate([a_t, r_t], axis=0).astype(BF16)
        ar_s[q, c] = ar
        bk = jnp.concatenate([_stack_heads(t["bv"] * inv), _stack_heads(t["k2"] * inv)], axis=0)
        gram = _mm_nt(ar, bk)
        yield
        zero = jnp.zeros((c_len, n4), F32)
        a_ab = jnp.where(strict, gram[:c_len, :n4], zero)
        a_ak = jnp.where(strict, gram[:c_len, n4:], zero)
        a_rb = jnp.where(incl, gram[c_len:, :n4], zero)
        a_rk = jnp.where(incl, gram[c_len:, n4:], zero)
        arb_s[q, c] = a_rb.astype(BF16)
        xs[q, c] = jnp.where(blockdiag, jnp.concatenate([a_ab] * RW_HEADS, axis=0), 0.0)
        av_s[q, c] = _mm(jnp.concatenate([a_ak, a_rk], axis=0), _stack_heads(t["v"]))
        yield
        bkh = jnp.concatenate([t["bv"] * tail, t["k2"] * tail], axis=0)
        bkh_s[q, c] = bkh.T.astype(BF16)
        dec_s[q, c] = jnp.broadcast_to(jnp.exp(last), (LANES, w)).T
        yield

    def invert(q, c):
        x = xs.pop((q, c))
        p_ = eye + x
        y_ = _mm(x, x)
        yield
        for lvl in range(levels):
            if lvl < levels - 1:
                z_ = _mm(jnp.concatenate([p_, y_], axis=0), y_)
                p_ = p_ + z_[:n4, :]
                y_ = z_[n4:, :]
            else:
                p_ = p_ + _mm(p_, y_)
            yield
        tcat_s[q, c] = rowblock_sum(p_).astype(BF16)

    def sequential(q, chunks):
        state = states[q]
        for c in chunks:
            sl = slice(c * c_len, (c + 1) * c_len)
            av = av_s[q, c]
            xy = _mm(ar_s[q, c], state)
            yield
            z_c = xy[:c_len, :] + av[:c_len, :]
            u_c = _mm(tcat_s[q, c], _stack_heads(z_c))
            yield
            y_s[q, sl, :] = xy[c_len:, :] + av[c_len:, :] + _mm(arb_s[q, c], _stack_heads(u_c))
            upd = _mm(bkh_s[q, c], jnp.concatenate([u_c, tok[q, c]["v"]], axis=0))
            state = state * dec_s[q, c][:, 0:1] + jnp.where(blockdiag, upd, 0.0)
            yield
        states[q] = state

    def finish(q, group):
        lo, hi = group[0] * c_len, (group[-1] + 1) * c_len
        cat = lambda name: jnp.concatenate([tok[q, c][name] for c in group], axis=0)
        r, k2, v, g = cat("r"), cat("k2"), cat("v"), cat("g")
        y = y_s[q, lo:hi, :]
        mean = _mm(y, head_ones) * (1.0 / RW_DH)
        bonus = _mm(r * k2 * rk_ref[...], head_ones) * v
        yield
        yc = y - mean
        var = _mm(yc * yc, head_ones) * (1.0 / RW_DH)
        yield
        yn = yc * lax.rsqrt(var + RW_GN_EPS) * lnw_ref[...] + lnb_ref[...]
        o_ref[q, lo:hi, :] = ((yn + bonus) * g).astype(BF16)
        yield

    def emit(activities):
        live = list(activities)
        while live:
            for act in list(live):
                if next(act, StopIteration) is StopIteration:
                    live.remove(act)

    seqs = range(u_ref.shape[0])
    states = [state_ref[q] for q in seqs]
    n_g = len(groups)
    for step in range(n_g + 4):
        acts = []
        for q in seqs:
            if step < n_g:
                acts.append(shift_and_heads(q, groups[step]))
            if 0 <= step - 1 < n_g:
                acts += [front(q, c) for c in groups[step - 1]]
            if 0 <= step - 2 < n_g:
                acts += [invert(q, c) for c in groups[step - 2]]
            if 0 <= step - 3 < n_g:
                acts.append(sequential(q, groups[step - 3]))
            if 0 <= step - 4 < n_g:
                acts.append(finish(q, groups[step - 4]))
        emit(acts)
    for q in seqs:
        state_ref[q] = states[q]
        carry_ref[q] = u_ref[q, ts - 1:ts, :]


def _rwkv(u, mu, w0, w2, a0, a2, g2, k_k, k_a, r_k, ln_w, ln_b, *, batch, seq, ts):
    w = RW_WIDTH
    ns = seq // ts
    nq = RW_SEQS if batch % RW_SEQS == 0 else 1
    zeros = lambda r, c: jnp.zeros((r, c), F32)
    w_lr = jnp.concatenate([jnp.concatenate([w2, zeros(RW_DECAY_RANK, w)], axis=1),
                            jnp.concatenate([zeros(RW_A_RANK, w), a2], axis=1)], axis=0)
    w_ab = jnp.concatenate([w0, a0]).reshape(1, 2 * w)
    vec = lambda p: p.reshape(1, w)
    nc = ts // RW_CHUNK
    c2 = 2 * RW_CHUNK
    scratch = [pltpu.VMEM((nq, 1, RW_COLS), F32), pltpu.VMEM((nq, w, w), F32),
               pltpu.VMEM((nq, ts, w), F32),
               pltpu.VMEM((nq, nc, c2, w), BF16), pltpu.VMEM((nq, nc, RW_CHUNK, w), BF16),
               pltpu.VMEM((nq, nc, RW_CHUNK, w), BF16), pltpu.VMEM((nq, nc, c2, w), F32),
               pltpu.VMEM((nq, nc, w, c2), BF16), pltpu.VMEM((nq, nc, w, LANES), F32)]
    out = pl.pallas_call(
        functools.partial(_rwkv_kernel, ts=ts),
        grid=(batch // nq, ns),
        in_specs=[pl.BlockSpec((nq, ts, RW_COLS), lambda b, i: (b, i, 0)),
                  _resident((1, RW_COLS)), _resident((LANES, 2 * w)), _resident((1, 2 * w)),
                  _resident((RW_GATE_RANK, w))] + [_resident((1, w))] * 5,
        out_specs=pl.BlockSpec((nq, ts, w), lambda b, i: (b, i, 0)),
        out_shape=jax.ShapeDtypeStruct((batch, seq, w), BF16),
        scratch_shapes=scratch,
        compiler_params=_cparams("parallel", "arbitrary"),
        name="rwkv7",
    )(u.reshape(batch, seq, RW_COLS), mu.reshape(1, RW_COLS), w_lr.astype(BF16), w_ab,
      g2.astype(BF16), vec(k_k), vec(k_a), vec(r_k), vec(ln_w), vec(ln_b))
    return out.reshape(batch * seq, w)


def _mem_kv_kernel(mem_ref, g_ref, w_ref, kv_ref):
    h = _rms(mem_ref[...], g_ref[...]).astype(BF16)
    kv_ref[...] = jnp.dot(h, w_ref[...], preferred_element_type=F32).astype(BF16)


def _mem_kv(mem, gain, w_kv, *, batch, mem_len):
    d = mem.shape[-1]
    n = w_kv.shape[1]
    return pl.pallas_call(
        _mem_kv_kernel,
        grid=(batch,),
        in_specs=[pl.BlockSpec((mem_len, d), lambda b: (b, 0)), _resident((1, d)),
                  _resident((d, n))],
        out_specs=pl.BlockSpec((mem_len, n), lambda b: (b, 0)),
        out_shape=jax.ShapeDtypeStruct((batch * mem_len, n), BF16),
        compiler_params=_cparams("parallel"),
        name="mem_kv",
    )(mem.reshape(batch * mem_len, d), gain.reshape(1, d), w_kv.astype(BF16))


def _mem_attn_kernel(q_ref, kv_ref, o_ref):
    q = q_ref[...]
    k = kv_ref[:, 0:XA_WIDTH]
    v = kv_ref[:, XA_WIDTH:2 * XA_WIDTH]
    zero = jnp.zeros_like(q)
    out = jnp.zeros(q.shape, F32)
    for h in range(XA_HEADS):
        hm = _head_mask(h, XA_WIDTH)
        s = _mm_nt(jnp.where(hm, q, zero), k)
        p = jnp.exp(s - jnp.max(s, axis=-1, keepdims=True))
        o = jnp.dot(p.astype(BF16), v, preferred_element_type=F32)
        out = jnp.where(hm, o / jnp.sum(p, axis=-1, keepdims=True), out)
    o_ref[...] = out.astype(BF16)


def _mem_attn(qx, kv, *, batch, seq, mem_len, tq):
    nq = seq // tq
    spec = pl.BlockSpec((tq, XA_WIDTH), lambda b, i: (b * nq + i, 0))
    return pl.pallas_call(
        _mem_attn_kernel,
        grid=(batch, nq),
        in_specs=[spec, pl.BlockSpec((mem_len, 2 * XA_WIDTH), lambda b, i: (b, 0))],
        out_specs=spec,
        out_shape=jax.ShapeDtypeStruct(qx.shape, BF16),
        compiler_params=_cparams("parallel", "arbitrary"),
        name="mem_attn",
    )(qx, kv)


def _merge_kernel(x_ref, oa_ref, ob_ref, oc_ref, g_ref, wa_ref, wb_ref, wc_ref, wo_ref, o_ref):
    d = x_ref.shape[1]
    dot = functools.partial(jnp.dot, preferred_element_type=F32)
    merged = (g_ref[:, 0:d].astype(F32) * dot(oa_ref[...], wa_ref[...])
              + g_ref[:, d:2 * d].astype(F32) * dot(ob_ref[...], wb_ref[...])
              + g_ref[:, 2 * d:3 * d].astype(F32) * dot(oc_ref[...], wc_ref[...]))
    o_ref[...] = x_ref[...] + dot(merged.astype(BF16), wo_ref[...])


def _merge(x, oa, ob, oc, gates, wa, wb, wc, wo, *, tm):
    t, d = x.shape
    row = lambda w: pl.BlockSpec((tm, w), lambda i: (i, 0))
    return pl.pallas_call(
        _merge_kernel,
        grid=(t // tm,),
        in_specs=[row(d), row(DA_WIDTH), row(RW_WIDTH), row(XA_WIDTH), row(3 * d),
                  _resident(wa.shape), _resident(wb.shape), _resident(wc.shape),
                  _resident(wo.shape)],
        out_specs=row(d),
        out_shape=jax.ShapeDtypeStruct((t, d), F32),
        compiler_params=_cparams("parallel"),
        name="merge",
    )(x, oa, ob, oc, gates, wa.astype(BF16), wb.astype(BF16), wc.astype(BF16), wo.astype(BF16))


def _pick(n, pref):
    t = min(n, pref)
    while n % t:
        t //= 2
    return t


def kernel(x, mem, positions, ffn1_norm, ffn1_w1, ffn1_w3, ffn1_w2, mix_norm, w_in, gate_bias, da_lambda_q1, da_lambda_k1, da_lambda_q2, da_lambda_k2, da_subln, rw_mu, rw_w0, rw_w2, rw_a0, rw_a2, rw_g2, rw_k_k, rw_k_a, rw_r_k, rw_ln_w, rw_ln_b, mem_norm, w_mem_kv, w_up_a, w_up_b, w_up_c, w_out, ffn2_norm, ffn2_w1, ffn2_w3, ffn2_w2, final_norm):
    batch, seq, d = x.shape
    mem_len = mem.shape[1]
    depth = ffn1_norm.shape[0]
    t = batch * seq
    tm = _pick(seq, 512)
    d_ff = ffn1_w1.shape[-1]
    ff_chunk = d_ff // 4 if (d_ff // 4) % LANES == 0 else d_ff
    tq = _pick(seq, 1024)
    ts = _pick(seq, 512)

    cos_t, sin_a, sin_b = _rope_tables(positions, ts=_pick(t, 1024))
    xf = x.reshape(t, d)
    for l in range(depth):
        lam_init = 0.8 - 0.6 * math.exp(-0.3 * l)
        norm_after = mix_norm[l]
        x1, h = _ffn(xf, ffn1_norm[l], ffn1_w1[l], ffn1_w3[l], ffn1_w2[l], norm_after,
                     final=False, tm=tm, ff_chunk=ff_chunk)
        q, k, v, u, qx, gates = _in_proj(h, w_in[l], cos_t, sin_a, sin_b, gate_bias[l],
                                         batch=batch, seq=seq, tm=tm)
        lam = _lambda(da_lambda_q1[l], da_lambda_k1[l], da_lambda_q2[l], da_lambda_k2[l], lam_init)
        o_a = _diff_attn(lam, q, k, v, da_subln[l], batch=batch, seq=seq, tq=tq,
                         out_scale=1.0 - lam_init)
        o_b = _rwkv(u, rw_mu[l], rw_w0[l], rw_w2[l], rw_a0[l], rw_a2[l], rw_g2[l], rw_k_k[l],
                    rw_k_a[l], rw_r_k[l], rw_ln_w[l], rw_ln_b[l], batch=batch, seq=seq, ts=ts)
        kv = _mem_kv(mem, mem_norm[l], w_mem_kv[l], batch=batch, mem_len=mem_len)
        o_c = _mem_attn(qx, kv, batch=batch, seq=seq, mem_len=mem_len, tq=_pick(seq, 512))
        x2 = _merge(x1, o_a, o_b, o_c, gates, w_up_a[l], w_up_b[l], w_up_c[l], w_out[l], tm=tm)
        last = l == depth - 1
        gout = final_norm if last else ffn1_norm[l + 1]
        if last:
            xf = _ffn(x2, ffn2_norm[l], ffn2_w1[l], ffn2_w3[l], ffn2_w2[l], gout,
                      final=True, tm=tm, ff_chunk=ff_chunk)
        else:
            xf, _ = _ffn(x2, ffn2_norm[l], ffn2_w1[l], ffn2_w3[l], ffn2_w2[l], gout,
                         final=False, tm=tm, ff_chunk=ff_chunk)
    return xf.reshape(batch, seq, d)
```

```python
import functools
import math

import jax
import jax.numpy as jnp
from jax import lax
from jax.experimental import pallas as pl
from jax.experimental.pallas import tpu as pltpu

F32 = jnp.float32
BF16 = jnp.bfloat16

CHUNK = 64
DA_HEADS = 4
DA_DH = 64
DA_WIDTH = DA_HEADS * 2 * DA_DH
RW_HEADS = 4
RW_DH = 64
RW_WIDTH = RW_HEADS * RW_DH
RW_DECAY_RANK = 64
RW_A_RANK = 64
RW_GATE_RANK = 128
RW_COLS = 3 * RW_WIDTH + RW_DECAY_RANK + RW_A_RANK + RW_GATE_RANK
RW_CHUNK = 64
RW_CUM_PASSES = 2
RW_SEQS = 2
RW_GROUP = 2
XA_HEADS = 4
XA_DH = 64
XA_WIDTH = XA_HEADS * XA_DH
ROPE_THETA = 500000.0
ROT_DIM = DA_DH // 4
ROT_HALF = ROT_DIM // 2
EPS = 1e-6
RW_GN_EPS = 64e-5
LANES = 128
NEG_BIG = -1e30
ATT_TK = 256
ATT_TILES_PER_STEP = 4
ATT_LOOKAHEAD = 4
LOG2E = math.log2(math.e)
VMEM_LIMIT = 56 * 1024 * 1024


def _cparams(*sem):
    return pltpu.CompilerParams(dimension_semantics=sem, vmem_limit_bytes=VMEM_LIMIT)


def _resident(shape):
    return pl.BlockSpec(shape, lambda *_: (0,) * len(shape), pipeline_mode=pl.Buffered(1))


def _mm(a, b):
    return jnp.dot(a.astype(BF16), b.astype(BF16), preferred_element_type=F32)


def _mm_nt(a, b):
    return lax.dot_general(a.astype(BF16), b.astype(BF16), (((1,), (1,)), ((), ())),
                           preferred_element_type=F32)


def _split_bf16(a, passes):
    parts = []
    for _ in range(passes):
        part = a.astype(BF16)
        parts.append(part)
        a = a - part.astype(F32)
    return parts


def _mm_f32_left(m_bf16, a, passes):
    return sum(jnp.dot(m_bf16, p, preferred_element_type=F32) for p in _split_bf16(a, passes))


def _head_mask(h, width=RW_WIDTH):
    lane = lax.broadcasted_iota(jnp.int32, (1, width), 1)
    return (lane // RW_DH) == h


def _rms(x, gain):
    return x * lax.rsqrt(jnp.mean(x * x, axis=-1, keepdims=True) + EPS) * gain


def _swiglu_half_step(x, gin, w1_ref, w3_ref, w2_ref, ff_chunk):
    h = _rms(x, gin).astype(BF16)
    acc = jnp.zeros(x.shape, F32)
    for c in range(w1_ref.shape[1] // ff_chunk):
        sl = slice(c * ff_chunk, (c + 1) * ff_chunk)
        a = jnp.dot(h, w1_ref[:, sl], preferred_element_type=F32)
        b = jnp.dot(h, w3_ref[:, sl], preferred_element_type=F32)
        act = (a * jax.nn.sigmoid(a) * b).astype(BF16)
        acc = acc + jnp.dot(act, w2_ref[sl, :], preferred_element_type=F32)
    return x + 0.5 * acc


def _ffn_first_kernel(x_ref, gin_ref, w1_ref, w3_ref, w2_ref, gout_ref, y_ref, h_ref, *, ff_chunk):
    y = _swiglu_half_step(x_ref[...], gin_ref[...], w1_ref, w3_ref, w2_ref, ff_chunk)
    y_ref[...] = y
    h_ref[...] = _rms(y, gout_ref[...]).astype(BF16)


def _ffn_first(x, gin, w1, w3, w2, gout, *, tm, ff_chunk):
    t, d = x.shape
    d_ff = w1.shape[1]
    row = pl.BlockSpec((tm, d), lambda i: (i, 0))
    return pl.pallas_call(
        functools.partial(_ffn_first_kernel, ff_chunk=ff_chunk),
        grid=(t // tm,),
        in_specs=[row, _resident((1, d)), _resident((d, d_ff)), _resident((d, d_ff)),
                  _resident((d_ff, d)), _resident((1, d))],
        out_specs=(row, row),
        out_shape=(jax.ShapeDtypeStruct((t, d), F32), jax.ShapeDtypeStruct((t, d), BF16)),
        compiler_params=_cparams("parallel"),
        name="ffn_first",
    )(x, gin.reshape(1, d), w1.astype(BF16), w3.astype(BF16), w2.astype(BF16), gout.reshape(1, d))


def _merge_ffn_kernel(x_ref, oa_ref, ob_ref, oc_ref, g_ref, wa_ref, wb_ref, wc_ref, wo_ref,
                      gin_ref, w1_ref, w3_ref, w2_ref, gout_ref, o_ref, *, ff_chunk):
    d = x_ref.shape[1]
    dot = functools.partial(jnp.dot, preferred_element_type=F32)
    merged = (g_ref[:, 0:d].astype(F32) * dot(oa_ref[...], wa_ref[...])
              + g_ref[:, d:2 * d].astype(F32) * dot(ob_ref[...], wb_ref[...])
              + g_ref[:, 2 * d:3 * d].astype(F32) * dot(oc_ref[...], wc_ref[...]))
    x2 = x_ref[...] + dot(merged.astype(BF16), wo_ref[...])
    y = _swiglu_half_step(x2, gin_ref[...], w1_ref, w3_ref, w2_ref, ff_chunk)
    o_ref[...] = _rms(y, gout_ref[...])


def _merge_ffn(x, oa, ob, oc, gates, wa, wb, wc, wo, gin, w1, w3, w2, gout, *, tm, ff_chunk):
    t, d = x.shape
    d_ff = w1.shape[1]
    row = lambda w: pl.BlockSpec((tm, w), lambda i: (i, 0))
    bf = lambda a: a.astype(BF16)
    return pl.pallas_call(
        functools.partial(_merge_ffn_kernel, ff_chunk=ff_chunk),
        grid=(t // tm,),
        in_specs=[row(d), row(DA_WIDTH), row(RW_WIDTH), row(XA_WIDTH), row(3 * d),
                  _resident(wa.shape), _resident(wb.shape), _resident(wc.shape),
                  _resident(wo.shape), _resident((1, d)), _resident((d, d_ff)),
                  _resident((d, d_ff)), _resident((d_ff, d)), _resident((1, d))],
        out_specs=row(d),
        out_shape=jax.ShapeDtypeStruct((t, d), F32),
        compiler_params=_cparams("parallel"),
        name="merge_ffn",
    )(x, oa, ob, oc, gates, bf(wa), bf(wb), bf(wc), bf(wo), gin.reshape(1, d), bf(w1), bf(w3),
      bf(w2), gout.reshape(1, d))


def _rope_kernel(pos_ref, invf_ref, cos_ref, sina_ref, sinb_ref):
    pos = pos_ref[...].astype(F32)
    ang = invf_ref[...] * pos
    c = jnp.cos(ang)
    s = jnp.sin(ang)
    one = jnp.ones_like(c)
    zero = jnp.zeros_like(c)
    pad = DA_DH // ROT_HALF - 2

    def lanes(first, second, fill):
        head = [first, second] + [fill] * pad
        return jnp.concatenate(head * (LANES // DA_DH), axis=0).T

    cos_ref[...] = lanes(c, c, one)
    sina_ref[...] = lanes(-s, zero, zero)
    sinb_ref[...] = lanes(zero, s, zero)


def _rope_tables(positions, *, ts):
    t = positions.size
    pos = positions.reshape(t // ts, 1, ts)
    inv_freq = jnp.power(jnp.float32(ROPE_THETA),
                         -jnp.arange(ROT_HALF, dtype=F32) * (2.0 / ROT_DIM)).reshape(ROT_HALF, 1)
    out = jax.ShapeDtypeStruct((t, LANES), F32)
    spec = pl.BlockSpec((ts, LANES), lambda i: (i, 0))
    return pl.pallas_call(
        _rope_kernel,
        grid=(t // ts,),
        in_specs=[pl.BlockSpec((None, 1, ts), lambda i: (i, 0, 0)), _resident((ROT_HALF, 1))],
        out_specs=(spec, spec, spec),
        out_shape=(out, out, out),
        compiler_params=_cparams("parallel"),
        name="rope_tables",
    )(pos, inv_freq)


def _mem_attention(q, kv_ref):
    k = kv_ref[:, 0:XA_WIDTH]
    v = kv_ref[:, XA_WIDTH:2 * XA_WIDTH]
    zero = jnp.zeros_like(q)
    out = jnp.zeros(q.shape, F32)
    for h in range(XA_HEADS):
        hm = _head_mask(h, XA_WIDTH)
        s = _mm_nt(jnp.where(hm, q, zero), k)
        p = jnp.exp(s - jnp.max(s, axis=-1, keepdims=True))
        o = jnp.dot(p.astype(BF16), v, preferred_element_type=F32)
        out = jnp.where(hm, o / jnp.sum(p, axis=-1, keepdims=True), out)
    return out


def _in_proj_kernel(h_ref, w_ref, cos_ref, sina_ref, sinb_ref, gb_ref, kv_ref,
                    q_ref, k_ref, v_ref, u_ref, oc_ref, g_ref, *, d_model):
    h = h_ref[...]
    cos_t = cos_ref[...]
    sin_a = sina_ref[...]
    sin_b = sinb_ref[...]

    def seg(lo, width):
        return jnp.dot(h, w_ref[:, lo:lo + width], preferred_element_type=F32)

    def rope(t):
        nxt = pltpu.roll(t, LANES - ROT_HALF, axis=1)
        prv = pltpu.roll(t, ROT_HALF, axis=1)
        return t * cos_t + nxt * sin_a + prv * sin_b

    c = 0
    tm = h.shape[0]
    pair = 2 * LANES
    for j in range(DA_WIDTH // pair):
        t2 = seg(c + j * pair, pair)
        for e in range(2):
            t = rope(t2[:, e * LANES:(e + 1) * LANES]) * (DA_DH ** -0.5 * LOG2E)
            q_ref[(2 * j + e) * LANES:(2 * j + e + 1) * LANES, :] = t.T.astype(BF16)
    c += DA_WIDTH
    for j in range(DA_WIDTH // pair):
        t2 = seg(c + j * pair, pair)
        for e in range(2):
            t = rope(t2[:, e * LANES:(e + 1) * LANES])
            k_ref[:, (2 * j + e) * LANES:(2 * j + e + 1) * LANES] = t.astype(BF16)
    c += DA_WIDTH
    for j in range(DA_WIDTH // pair):
        t2 = seg(c + j * pair, pair)
        for e in range(2):
            for jj in range(tm // ATT_TK):
                v_ref[2 * j + e, jj] = t2[jj * ATT_TK:(jj + 1) * ATT_TK,
                                          e * LANES:(e + 1) * LANES].T.astype(BF16)
    c += DA_WIDTH
    u_ref[...] = seg(c, RW_COLS)
    c += RW_COLS
    qx = (seg(c, XA_WIDTH) * (XA_DH ** -0.5)).astype(BF16)
    oc_ref[...] = _mem_attention(qx, kv_ref).astype(BF16)
    c += XA_WIDTH
    for j in range(3):
        gl = seg(c + j * d_model, d_model) + gb_ref[:, j * d_model:(j + 1) * d_model]
        g_ref[:, j * d_model:(j + 1) * d_model] = jax.nn.sigmoid(gl).astype(BF16)


def _in_proj(h, w_in, cos_t, sin_a, sin_b, gate_bias, kv, *, batch, seq, tm):
    t, d = h.shape
    cols = w_in.shape[1]
    nst = seq // tm
    mem_len = kv.shape[0] // batch
    row = lambda w: pl.BlockSpec((tm, w), lambda i: (i, 0))
    sd = jax.ShapeDtypeStruct
    out_shape = (sd((batch * DA_WIDTH, seq), BF16), sd((t, DA_WIDTH), BF16),
                 sd((batch * DA_HEADS, seq // ATT_TK, LANES, ATT_TK), BF16),
                 sd((t, RW_COLS), F32), sd((t, XA_WIDTH), BF16), sd((t, 3 * d), BF16))
    out_specs = (pl.BlockSpec((DA_WIDTH, tm), lambda i: (i // nst, i % nst)),
                 row(DA_WIDTH),
                 pl.BlockSpec((DA_HEADS, tm // ATT_TK, LANES, ATT_TK),
                              lambda i: (i // nst, i % nst, 0, 0)),
                 row(RW_COLS), row(XA_WIDTH), row(3 * d))
    return pl.pallas_call(
        functools.partial(_in_proj_kernel, d_model=d),
        grid=(t // tm,),
        in_specs=[row(d), _resident((d, cols)), row(LANES), row(LANES), row(LANES),
                  _resident((1, 3 * d)),
                  pl.BlockSpec((mem_len, 2 * XA_WIDTH), lambda i: (i // nst, 0))],
        out_specs=out_specs,
        out_shape=out_shape,
        compiler_params=_cparams("parallel"),
        name="in_proj",
    )(h, w_in.astype(BF16), cos_t, sin_a, sin_b, gate_bias.reshape(1, 3 * d), kv)


def _lambda_kernel(q1_ref, k1_ref, q2_ref, k2_ref, out_ref, *, lam_init):
    l1 = jnp.exp(jnp.sum(q1_ref[...] * k1_ref[...], axis=-1, keepdims=True))
    l2 = jnp.exp(jnp.sum(q2_ref[...] * k2_ref[...], axis=-1, keepdims=True))
    out_ref[...] = jnp.broadcast_to(l1 - l2 + lam_init, out_ref.shape)


def _lambda(q1, k1, q2, k2, lam_init):
    vec = lambda a: a.reshape(1, DA_DH)
    out = pl.pallas_call(
        functools.partial(_lambda_kernel, lam_init=lam_init),
        out_shape=jax.ShapeDtypeStruct((8, LANES), F32),
        name="da_lambda",
    )(vec(q1), vec(k1), vec(q2), vec(k2))
    return out[0, :1]


def _diff_attn_kernel(lam_ref, q_ref, k_ref, v_ref, gain_ref, o_ref,
                      qm_s, m_s, l_s, acc_s, *, tq, out_scale):
    i = pl.program_id(2)
    lam = lam_ref[0]
    tk = ATT_TK
    ns = tq // tk
    q = q_ref[...]
    first = lax.broadcasted_iota(jnp.int32, (LANES, 1), 0) < DA_DH
    zero = jnp.zeros_like(q)
    qm_s[0] = jnp.where(first, q, zero)
    qm_s[1] = jnp.where(first, zero, q)
    m_s[...] = jnp.full(m_s.shape, NEG_BIG, F32)
    l_s[...] = jnp.zeros(l_s.shape, F32)
    acc_s[...] = jnp.zeros(acc_s.shape, F32)
    key_c = lax.broadcasted_iota(jnp.int32, (tk, tk), 0) // CHUNK
    qry_c = lax.broadcasted_iota(jnp.int32, (tk, tk), 1) // CHUNK
    causal = key_c <= qry_c

    def scores(unit):
        j, c, mi, _ = unit
        k = k_ref[pl.ds(pl.multiple_of(j * tk, tk), tk), :]
        return jnp.dot(k, qm_s[mi, :, c * tk:(c + 1) * tk], preferred_element_type=F32)

    def absorb(unit, s):
        j, c, mi, masked = unit
        cs = slice(c * tk, (c + 1) * tk)
        if masked:
            s = jnp.where(causal, s, NEG_BIG)
        m_old = m_s[mi, :, cs]
        m_new = jnp.maximum(m_old, jnp.max(s, axis=0, keepdims=True))
        alpha = jnp.exp2(m_old - m_new)
        p = jnp.exp2(s - m_new)
        m_s[mi, :, cs] = m_new
        l_s[mi, :, cs] = alpha * l_s[mi, :, cs] + jnp.sum(p, axis=0, keepdims=True)
        acc_s[mi, :, cs] = alpha * acc_s[mi, :, cs] + jnp.dot(
            v_ref[j], p.astype(BF16), preferred_element_type=F32)

    def run(units):
        pending = []
        for idx in range(len(units) + ATT_LOOKAHEAD):
            if idx < len(units):
                pending.append(scores(units[idx]))
            if idx >= ATT_LOOKAHEAD:
                absorb(units[idx - ATT_LOOKAHEAD], pending[idx - ATT_LOOKAHEAD])

    def body(jj, carry):
        run([(jj * ATT_TILES_PER_STEP + t, c, mi, False)
             for t in range(ATT_TILES_PER_STEP) for c in range(ns) for mi in range(2)])
        return carry

    lax.fori_loop(0, i * ns // ATT_TILES_PER_STEP, body, 0)
    run([(i * ns + d, c, mi, c == d)
         for d in range(ns) for c in range(d, ns) for mi in range(2)])
    o = acc_s[0] / l_s[0] - lam * (acc_s[1] / l_s[1])
    o = o * lax.rsqrt(jnp.mean(o * o, axis=0, keepdims=True) + EPS) * gain_ref[...]
    o_ref[...] = (o * out_scale).T.astype(BF16)


def _diff_attn(lam, q_t, k, v_t, gain, *, batch, seq, tq, out_scale):
    nq = seq // tq
    nk = seq // ATT_TK
    assert (tq // ATT_TK) % ATT_TILES_PER_STEP == 0
    return pl.pallas_call(
        functools.partial(_diff_attn_kernel, tq=tq, out_scale=out_scale),
        grid=(batch, DA_HEADS, nq),
        scratch_shapes=[pltpu.VMEM((2, LANES, tq), BF16), pltpu.VMEM((2, 1, tq), F32),
                        pltpu.VMEM((2, 1, tq), F32), pltpu.VMEM((2, LANES, tq), F32)],
        in_specs=[pl.BlockSpec(memory_space=pltpu.SMEM),
                  pl.BlockSpec((LANES, tq), lambda b, h, i: (b * DA_HEADS + h, i)),
                  pl.BlockSpec((seq, LANES), lambda b, h, i: (b, h)),
                  pl.BlockSpec((None, nk, LANES, ATT_TK),
                               lambda b, h, i: (b * DA_HEADS + h, 0, 0, 0)),
                  _resident((LANES, 1))],
        out_specs=pl.BlockSpec((tq, LANES), lambda b, h, i: (b * nq + i, h)),
        out_shape=jax.ShapeDtypeStruct(k.shape, BF16),
        compiler_params=_cparams("parallel", "parallel", "arbitrary"),
        name="diff_attn",
    )(lam, q_t, k, v_t, gain.reshape(LANES, 1))


def _stack_heads(x):
    zero = jnp.zeros_like(x)
    return jnp.concatenate([jnp.where(_head_mask(h), x, zero) for h in range(RW_HEADS)], axis=0)


def _rwkv_kernel(u_ref, mu_ref, wlr_ref, wab_ref, g2_ref, kk_ref, ka_ref, rk_ref,
                 lnw_ref, lnb_ref, o_ref,
                 carry_ref, state_ref, y_s, ar_s, arb_s, tcat_s, av_s, bkh_s, dec_s, *, ts):
    c_len = RW_CHUNK
    w = RW_WIDTH
    n4 = RW_HEADS * c_len
    n_chunks = ts // c_len
    groups = [range(g0, min(g0 + RW_GROUP, n_chunks)) for g0 in range(0, n_chunks, RW_GROUP)]

    @pl.when(pl.program_id(1) == 0)
    def _():
        carry_ref[...] = jnp.zeros_like(carry_ref)
        state_ref[...] = jnp.zeros_like(state_ref)

    hr = lax.broadcasted_iota(jnp.int32, (w, w), 0) // RW_DH
    hc = lax.broadcasted_iota(jnp.int32, (w, w), 1) // RW_DH
    head_ones = (hr == hc).astype(BF16)
    tr = lax.broadcasted_iota(jnp.int32, (c_len, c_len), 0)
    tc = lax.broadcasted_iota(jnp.int32, (c_len, c_len), 1)
    tri = (tc <= tr).astype(BF16)
    tt = lax.broadcasted_iota(jnp.int32, (c_len, n4), 0)
    ss = lax.broadcasted_iota(jnp.int32, (c_len, n4), 1) % c_len
    strict = ss < tt
    incl = ss <= tt
    br = lax.broadcasted_iota(jnp.int32, (n4, n4), 0)
    bc = lax.broadcasted_iota(jnp.int32, (n4, n4), 1)
    blockdiag = (br // c_len) == (bc // c_len)
    eye = (br == bc).astype(F32)
    levels = int(math.log2(c_len)) - 1
    lane = lax.broadcasted_iota(jnp.int32, (1, LANES), 1)

    def rowblock_sum(x):
        return sum(x[h * c_len:(h + 1) * c_len, :] for h in range(RW_HEADS))

    tok = {}
    xs = {}

    def shift_and_heads(q, group):
        lo, hi = group[0] * c_len, (group[-1] + 1) * c_len
        n = hi - lo
        u = u_ref[q, lo:hi, :]
        before = carry_ref[q] if lo == 0 else u_ref[q, lo - 1:lo, :]
        row = lax.broadcasted_iota(jnp.int32, (n, 1), 0)
        prev = jnp.where(row == 0, before, pltpu.roll(u, 1, axis=0))
        us = u + (prev - u) * mu_ref[...]
        r = us[:, 0:w]
        k = us[:, w:2 * w]
        v = us[:, 2 * w:3 * w]
        low = us[:, 3 * w:3 * w + LANES]
        low = jnp.where(lane < RW_DECAY_RANK, jnp.tanh(low), low)
        lr = _mm(low, wlr_ref[...])
        g = _mm(jax.nn.sigmoid(us[:, 3 * w + LANES:3 * w + LANES + RW_GATE_RANK]), g2_ref[...])
        kk = k * kk_ref[...]
        ksq = _mm(kk * kk, head_ones)
        yield
        z = -(wab_ref[:, 0:w] + lr[:, 0:w])
        softplus = jnp.maximum(z, 0.0) + jnp.log(1.0 + jnp.exp(-jnp.abs(z)))
        lw = -jnp.exp(-softplus - 0.5)
        a = jax.nn.sigmoid(wab_ref[:, w:2 * w] + lr[:, w:2 * w])
        kk = kk / jnp.maximum(jnp.sqrt(ksq), 1e-12)
        k2 = k * (1.0 + (a - 1.0) * ka_ref[...])
        for i, c in enumerate(group):
            sl = slice(i * c_len, (i + 1) * c_len)
            tok[q, c] = dict(r=r[sl], k2=k2[sl], v=v[sl], lw=lw[sl], kk=kk[sl],
                             bv=kk[sl] * a[sl], g=g[sl])
        yield

    def front(q, c):
        t = tok[q, c]
        cum_c = _mm_f32_left(tri, t["lw"], RW_CUM_PASSES)
        yield
        last = cum_c[c_len - 1:c_len, :]
        inv = jnp.exp(-cum_c)
        tail = jnp.exp(last - cum_c)
        a_t = -t["kk"] * jnp.exp(cum_c - t["lw"])
        r_t = t["r"] * jnp.exp(cum_c)
        ar = jnp.concatenate([a_t, r_t], axis=0).astype(BF16)
        ar_s[q, c] = ar
        bk = jnp.concatenate([_stack_heads(t["bv"] * inv), _stack_heads(t["k2"] * inv)], axis=0)
        gram = _mm_nt(ar, bk)
        yield
        zero = jnp.zeros((c_len, n4), F32)
        a_ab = jnp.where(strict, gram[:c_len, :n4], zero)
        a_ak = jnp.where(strict, gram[:c_len, n4:], zero)
        a_rb = jnp.where(incl, gram[c_len:, :n4], zero)
        a_rk = jnp.where(incl, gram[c_len:, n4:], zero)
        arb_s[q, c] = a_rb.astype(BF16)
        xs[q, c] = jnp.where(blockdiag, jnp.concatenate([a_ab] * RW_HEADS, axis=0), 0.0)
        av_s[q, c] = _mm(jnp.concatenate([a_ak, a_rk], axis=0), _stack_heads(t["v"]))
        yield
        bkh = jnp.concatenate([t["bv"] * tail, t["k2"] * tail], axis=0)
        bkh_s[q, c] = bkh.T.astype(BF16)
        dec_s[q, c] = jnp.broadcast_to(jnp.exp(last), (LANES, w)).T
        yield

    def invert(q, c):
        x = xs.pop((q, c))
        p_ = eye + x
        y_ = _mm(x, x)
        yield
        for lvl in range(levels):
            if lvl < levels - 1:
                z_ = _mm(jnp.concatenate([p_, y_], axis=0), y_)
                p_ = p_ + z_[:n4, :]
                y_ = z_[n4:, :]
            else:
                p_ = p_ + _mm(p_, y_)
            yield
        tcat_s[q, c] = rowblock_sum(p_).astype(BF16)

    def sequential(q, chunks):
        state = states[q]
        for c in chunks:
            sl = slice(c * c_len, (c + 1) * c_len)
            av = av_s[q, c]
            xy = _mm(ar_s[q, c], state)
            yield
            z_c = xy[:c_len, :] + av[:c_len, :]
            u_c = _mm(tcat_s[q, c], _stack_heads(z_c))
            yield
            y_s[q, sl, :] = xy[c_len:, :] + av[c_len:, :] + _mm(arb_s[q, c], _stack_heads(u_c))
            upd = _mm(bkh_s[q, c], jnp.concatenate([u_c, tok[q, c]["v"]], axis=0))
            state = state * dec_s[q, c][:, 0:1] + jnp.where(blockdiag, upd, 0.0)
            yield
        states[q] = state

    def finish(q, group):
        lo, hi = group[0] * c_len, (group[-1] + 1) * c_len
        cat = lambda name: jnp.concatenate([tok[q, c][name] for c in group], axis=0)
        r, k2, v, g = cat("r"), cat("k2"), cat("v"), cat("g")
        y = y_s[q, lo:hi, :]
        mean = _mm(y, head_ones) * (1.0 / RW_DH)
        bonus = _mm(r * k2 * rk_ref[...], head_ones) * v
        yield
        yc = y - mean
        var = _mm(yc * yc, head_ones) * (1.0 / RW_DH)
        yield
        yn = yc * lax.rsqrt(var + RW_GN_EPS) * lnw_ref[...] + lnb_ref[...]
        o_ref[q, lo:hi, :] = ((yn + bonus) * g).astype(BF16)
        yield

    def emit(activities):
        live = list(activities)
        while live:
            for act in list(live):
                if next(act, StopIteration) is StopIteration:
                    live.remove(act)

    seqs = range(u_ref.shape[0])
    states = [state_ref[q] for q in seqs]
    n_g = len(groups)
    for step in range(n_g + 4):
        acts = []
        for q in seqs:
            if step < n_g:
                acts.append(shift_and_heads(q, groups[step]))
            if 0 <= step - 1 < n_g:
                acts += [front(q, c) for c in groups[step - 1]]
            if 0 <= step - 2 < n_g:
                acts += [invert(q, c) for c in groups[step - 2]]
            if 0 <= step - 3 < n_g:
                acts.append(sequential(q, groups[step - 3]))
            if 0 <= step - 4 < n_g:
                acts.append(finish(q, groups[step - 4]))
        emit(acts)
    for q in seqs:
        state_ref[q] = states[q]
        carry_ref[q] = u_ref[q, ts - 1:ts, :]


def _rwkv(u, mu, w0, w2, a0, a2, g2, k_k, k_a, r_k, ln_w, ln_b, *, batch, seq, ts):
    w = RW_WIDTH
    ns = seq // ts
    nq = RW_SEQS if batch % RW_SEQS == 0 else 1
    zeros = lambda r, c: jnp.zeros((r, c), F32)
    w_lr = jnp.concatenate([jnp.concatenate([w2, zeros(RW_DECAY_RANK, w)], axis=1),
                            jnp.concatenate([zeros(RW_A_RANK, w), a2], axis=1)], axis=0)
    w_ab = jnp.concatenate([w0, a0]).reshape(1, 2 * w)
    vec = lambda p: p.reshape(1, w)
    nc = ts // RW_CHUNK
    c2 = 2 * RW_CHUNK
    scratch = [pltpu.VMEM((nq, 1, RW_COLS), F32), pltpu.VMEM((nq, w, w), F32),
               pltpu.VMEM((nq, ts, w), F32),
               pltpu.VMEM((nq, nc, c2, w), BF16), pltpu.VMEM((nq, nc, RW_CHUNK, w), BF16),
               pltpu.VMEM((nq, nc, RW_CHUNK, w), BF16), pltpu.VMEM((nq, nc, c2, w), F32),
               pltpu.VMEM((nq, nc, w, c2), BF16), pltpu.VMEM((nq, nc, w, LANES), F32)]
    out = pl.pallas_call(
        functools.partial(_rwkv_kernel, ts=ts),
        grid=(batch // nq, ns),
        in_specs=[pl.BlockSpec((nq, ts, RW_COLS), lambda b, i: (b, i, 0)),
                  _resident((1, RW_COLS)), _resident((LANES, 2 * w)), _resident((1, 2 * w)),
                  _resident((RW_GATE_RANK, w))] + [_resident((1, w))] * 5,
        out_specs=pl.BlockSpec((nq, ts, w), lambda b, i: (b, i, 0)),
        out_shape=jax.ShapeDtypeStruct((batch, seq, w), BF16),
        scratch_shapes=scratch,
        compiler_params=_cparams("parallel", "arbitrary"),
        name="rwkv7",
    )(u.reshape(batch, seq, RW_COLS), mu.reshape(1, RW_COLS), w_lr.astype(BF16), w_ab,
      g2.astype(BF16), vec(k_k), vec(k_a), vec(r_k), vec(ln_w), vec(ln_b))
    return out.reshape(batch * seq, w)


def _mem_kv_kernel(mem_ref, g_ref, w_ref, kv_ref):
    h = _rms(mem_ref[...], g_ref[...]).astype(BF16)
    kv_ref[...] = jnp.dot(h, w_ref[...], preferred_element_type=F32).astype(BF16)


def _mem_kv(mem, gain, w_kv, *, batch, mem_len):
    d = mem.shape[-1]
    n = w_kv.shape[1]
    return pl.pallas_call(
        _mem_kv_kernel,
        grid=(batch,),
        in_specs=[pl.BlockSpec((mem_len, d), lambda b: (b, 0)), _resident((1, d)),
                  _resident((d, n))],
        out_specs=pl.BlockSpec((mem_len, n), lambda b: (b, 0)),
        out_shape=jax.ShapeDtypeStruct((batch * mem_len, n), BF16),
        compiler_params=_cparams("parallel"),
        name="mem_kv",
    )(mem.reshape(batch * mem_len, d), gain.reshape(1, d), w_kv.astype(BF16))


def _pick(n, pref):
    t = min(n, pref)
    while n % t:
        t //= 2
    return t


def kernel(x, mem, positions, ffn1_norm, ffn1_w1, ffn1_w3, ffn1_w2, mix_norm, w_in, gate_bias, da_lambda_q1, da_lambda_k1, da_lambda_q2, da_lambda_k2, da_subln, rw_mu, rw_w0, rw_w2, rw_a0, rw_a2, rw_g2, rw_k_k, rw_k_a, rw_r_k, rw_ln_w, rw_ln_b, mem_norm, w_mem_kv, w_up_a, w_up_b, w_up_c, w_out, ffn2_norm, ffn2_w1, ffn2_w3, ffn2_w2, final_norm):
    batch, seq, d = x.shape
    mem_len = mem.shape[1]
    depth = ffn1_norm.shape[0]
    t = batch * seq
    tm = _pick(seq, 512)
    d_ff = ffn1_w1.shape[-1]
    ff_chunk = d_ff
    tq = _pick(seq, 1024)
    ts = _pick(seq, 512)

    cos_t, sin_a, sin_b = _rope_tables(positions, ts=_pick(t, 1024))
    xf = x.reshape(t, d)
    for l in range(depth):
        lam_init = 0.8 - 0.6 * math.exp(-0.3 * l)
        x1, h = _ffn_first(xf, ffn1_norm[l], ffn1_w1[l], ffn1_w3[l], ffn1_w2[l], mix_norm[l],
                           tm=tm, ff_chunk=ff_chunk)
        kv = _mem_kv(mem, mem_norm[l], w_mem_kv[l], batch=batch, mem_len=mem_len)
        q, k, v, u, o_c, gates = _in_proj(h, w_in[l], cos_t, sin_a, sin_b, gate_bias[l], kv,
                                          batch=batch, seq=seq, tm=tm)
        lam = _lambda(da_lambda_q1[l], da_lambda_k1[l], da_lambda_q2[l], da_lambda_k2[l], lam_init)
        o_a = _diff_attn(lam, q, k, v, da_subln[l], batch=batch, seq=seq, tq=tq,
                         out_scale=1.0 - lam_init)
        o_b = _rwkv(u, rw_mu[l], rw_w0[l], rw_w2[l], rw_a0[l], rw_a2[l], rw_g2[l], rw_k_k[l],
                    rw_k_a[l], rw_r_k[l], rw_ln_w[l], rw_ln_b[l], batch=batch, seq=seq, ts=ts)
        assert depth == 1, "stacked layers need the un-normalised output of the second half step"
        xf = _merge_ffn(x1, o_a, o_b, o_c, gates, w_up_a[l], w_up_b[l], w_up_c[l], w_out[l],
                        ffn2_norm[l], ffn2_w1[l], ffn2_w3[l], ffn2_w2[l], final_norm,
                        tm=tm, ff_chunk=ff_chunk)
    return xf.reshape(batch, seq, d)
```

```python
import functools
import math

import jax
import jax.numpy as jnp
from jax import lax
from jax.experimental import pallas as pl
from jax.experimental.pallas import tpu as pltpu

F32 = jnp.float32
BF16 = jnp.bfloat16

CHUNK = 64
DA_HEADS = 4
DA_DH = 64
DA_WIDTH = DA_HEADS * 2 * DA_DH
RW_HEADS = 4
RW_DH = 64
RW_WIDTH = RW_HEADS * RW_DH
RW_DECAY_RANK = 64
RW_A_RANK = 64
RW_GATE_RANK = 128
RW_COLS = 3 * RW_WIDTH + RW_DECAY_RANK + RW_A_RANK + RW_GATE_RANK
RW_CHUNK = 64
RW_CUM_PASSES = 2
RW_SEQS = 2
RW_GROUP = 2
XA_HEADS = 4
XA_DH = 64
XA_WIDTH = XA_HEADS * XA_DH
ROPE_THETA = 500000.0
ROT_DIM = DA_DH // 4
ROT_HALF = ROT_DIM // 2
EPS = 1e-6
RW_GN_EPS = 64e-5
LANES = 128
NEG_BIG = -1e30
ATT_TK = 256
ATT_TILES_PER_STEP = 4
ATT_VROWS = LANES + 16
ATT_LOOKAHEAD = 4
LOG2E = math.log2(math.e)
FFN_ROW_PARTS = 2
VMEM_LIMIT = 56 * 1024 * 1024


def _cparams(*sem, flags=None):
    return pltpu.CompilerParams(dimension_semantics=sem, vmem_limit_bytes=VMEM_LIMIT, flags=flags)


def _resident(shape):
    return pl.BlockSpec(shape, lambda *_: (0,) * len(shape), pipeline_mode=pl.Buffered(1))


def _mm(a, b):
    return jnp.dot(a.astype(BF16), b.astype(BF16), preferred_element_type=F32)


def _mm_nt(a, b):
    return lax.dot_general(a.astype(BF16), b.astype(BF16), (((1,), (1,)), ((), ())),
                           preferred_element_type=F32)


def _split_bf16(a, passes):
    parts = []
    for _ in range(passes):
        part = a.astype(BF16)
        parts.append(part)
        a = a - part.astype(F32)
    return parts


def _mm_f32_left(m_bf16, a, passes):
    return sum(jnp.dot(m_bf16, p, preferred_element_type=F32) for p in _split_bf16(a, passes))


def _head_mask(h, width=RW_WIDTH):
    lane = lax.broadcasted_iota(jnp.int32, (1, width), 1)
    return (lane // RW_DH) == h


def _rms(x, gain):
    return x * lax.rsqrt(jnp.mean(x * x, axis=-1, keepdims=True) + EPS) * gain


def _swiglu_half_step(x, gin, w1_ref, w3_ref, w2_ref):
    rows = x.shape[0]
    parts = FFN_ROW_PARTS if rows % (FFN_ROW_PARTS * 16) == 0 else 1
    n = rows // parts
    hs = [_rms(x[p * n:(p + 1) * n], gin).astype(BF16) for p in range(parts)]
    pre = [(jnp.dot(h, w1_ref[...], preferred_element_type=F32),
            jnp.dot(h, w3_ref[...], preferred_element_type=F32)) for h in hs]
    outs = []
    for p, (a, b) in enumerate(pre):
        act = (a * jax.nn.sigmoid(a) * b).astype(BF16)
        outs.append(x[p * n:(p + 1) * n] + 0.5 * jnp.dot(act, w2_ref[...],
                                                         preferred_element_type=F32))
    return jnp.concatenate(outs, axis=0)


def _ffn_first_kernel(x_ref, gin_ref, w1_ref, w3_ref, w2_ref, gout_ref, y_ref, h_ref):
    y = _swiglu_half_step(x_ref[...], gin_ref[...], w1_ref, w3_ref, w2_ref)
    y_ref[...] = y
    h_ref[...] = _rms(y, gout_ref[...]).astype(BF16)


def _ffn_first(x, gin, w1, w3, w2, gout, *, tm):
    t, d = x.shape
    d_ff = w1.shape[1]
    row = pl.BlockSpec((tm, d), lambda i: (i, 0))
    return pl.pallas_call(
        _ffn_first_kernel,
        grid=(t // tm,),
        in_specs=[row, _resident((1, d)), _resident((d, d_ff)), _resident((d, d_ff)),
                  _resident((d_ff, d)), _resident((1, d))],
        out_specs=(row, row),
        out_shape=(jax.ShapeDtypeStruct((t, d), F32), jax.ShapeDtypeStruct((t, d), BF16)),
        compiler_params=_cparams("parallel"),
        name="ffn_first",
    )(x, gin.reshape(1, d), w1.astype(BF16), w3.astype(BF16), w2.astype(BF16), gout.reshape(1, d))


def _merge_ffn_kernel(x_ref, oa_ref, ob_ref, oc_ref, g_ref, wa_ref, wb_ref, wc_ref, wo_ref,
                      gin_ref, w1_ref, w3_ref, w2_ref, gout_ref, o_ref):
    d = x_ref.shape[1]
    dot = functools.partial(jnp.dot, preferred_element_type=F32)
    merged = (g_ref[:, 0:d].astype(F32) * dot(oa_ref[...], wa_ref[...])
              + g_ref[:, d:2 * d].astype(F32) * dot(ob_ref[...], wb_ref[...])
              + g_ref[:, 2 * d:3 * d].astype(F32) * dot(oc_ref[...], wc_ref[...]))
    x2 = x_ref[...] + dot(merged.astype(BF16), wo_ref[...])
    y = _swiglu_half_step(x2, gin_ref[...], w1_ref, w3_ref, w2_ref)
    o_ref[...] = _rms(y, gout_ref[...])


def _merge_ffn(x, oa, ob, oc, gates, wa, wb, wc, wo, gin, w1, w3, w2, gout, *, tm):
    t, d = x.shape
    d_ff = w1.shape[1]
    row = lambda w: pl.BlockSpec((tm, w), lambda i: (i, 0))
    bf = lambda a: a.astype(BF16)
    return pl.pallas_call(
        _merge_ffn_kernel,
        grid=(t // tm,),
        in_specs=[row(d), row(DA_WIDTH), row(RW_WIDTH), row(XA_WIDTH), row(3 * d),
                  _resident(wa.shape), _resident(wb.shape), _resident(wc.shape),
                  _resident(wo.shape), _resident((1, d)), _resident((d, d_ff)),
                  _resident((d, d_ff)), _resident((d_ff, d)), _resident((1, d))],
        out_specs=row(d),
        out_shape=jax.ShapeDtypeStruct((t, d), F32),
        compiler_params=_cparams("parallel"),
        name="merge_ffn",
    )(x, oa, ob, oc, gates, bf(wa), bf(wb), bf(wc), bf(wo), gin.reshape(1, d), bf(w1), bf(w3),
      bf(w2), gout.reshape(1, d))


def _rope_tables(pos, inv_freq):
    ang = inv_freq * pos.astype(F32)
    c = jnp.cos(ang)
    s = jnp.sin(ang)
    one = jnp.ones_like(c)
    zero = jnp.zeros_like(c)
    pad = DA_DH // ROT_HALF - 2

    def lanes(first, second, fill):
        head = [first, second] + [fill] * pad
        return jnp.concatenate(head * (LANES // DA_DH), axis=0).T

    return lanes(c, c, one), lanes(-s, zero, zero), lanes(zero, s, zero)


def _mem_attention(q, kv_ref):
    k = kv_ref[:, 0:XA_WIDTH]
    v = kv_ref[:, XA_WIDTH:2 * XA_WIDTH]
    zero = jnp.zeros_like(q)
    out = jnp.zeros(q.shape, F32)
    for h in range(XA_HEADS):
        hm = _head_mask(h, XA_WIDTH)
        s = _mm_nt(jnp.where(hm, q, zero), k)
        p = jnp.exp(s - jnp.max(s, axis=-1, keepdims=True))
        o = jnp.dot(p.astype(BF16), v, preferred_element_type=F32)
        out = jnp.where(hm, o / jnp.sum(p, axis=-1, keepdims=True), out)
    return out


def _in_proj_kernel(h_ref, w_ref, pos_ref, invf_ref, gb_ref, kv_ref,
                    q_ref, k_ref, v_ref, u_ref, oc_ref, g_ref, *, d_model):
    h = h_ref[...]
    cos_t, sin_a, sin_b = _rope_tables(pos_ref[...], invf_ref[...])

    def seg(lo, width):
        return jnp.dot(h, w_ref[:, lo:lo + width], preferred_element_type=F32)

    def rope(t):
        nxt = pltpu.roll(t, LANES - ROT_HALF, axis=1)
        prv = pltpu.roll(t, ROT_HALF, axis=1)
        return t * cos_t + nxt * sin_a + prv * sin_b

    tm = h.shape[0]
    pair = 2 * LANES
    c_q, c_k, c_v = 0, DA_WIDTH, 2 * DA_WIDTH
    c_u = 3 * DA_WIDTH
    c_x = c_u + RW_COLS
    c_g = c_x + XA_WIDTH
    for j in range(3):
        gl = seg(c_g + j * d_model, d_model) + gb_ref[:, j * d_model:(j + 1) * d_model]
        g_ref[:, j * d_model:(j + 1) * d_model] = jax.nn.sigmoid(gl).astype(BF16)
    qx = (seg(c_x, XA_WIDTH) * (XA_DH ** -0.5)).astype(BF16)
    oc_ref[...] = _mem_attention(qx, kv_ref).astype(BF16)
    for j in range(DA_WIDTH // pair):
        t2 = seg(c_q + j * pair, pair)
        for e in range(2):
            t = rope(t2[:, e * LANES:(e + 1) * LANES]) * (DA_DH ** -0.5 * LOG2E)
            q_ref[(2 * j + e) * LANES:(2 * j + e + 1) * LANES, :] = t.T.astype(BF16)
    for j in range(DA_WIDTH // pair):
        t2 = seg(c_k + j * pair, pair)
        for e in range(2):
            t = rope(t2[:, e * LANES:(e + 1) * LANES])
            k_ref[:, (2 * j + e) * LANES:(2 * j + e + 1) * LANES] = t.astype(BF16)
    for j in range(DA_WIDTH // pair):
        t2 = seg(c_v + j * pair, pair)
        for e in range(2):
            for jj in range(tm // ATT_TK):
                v_ref[2 * j + e, jj, 0:LANES, :] = t2[jj * ATT_TK:(jj + 1) * ATT_TK,
                                                      e * LANES:(e + 1) * LANES].T.astype(BF16)
                v_ref[2 * j + e, jj, LANES:ATT_VROWS, :] = jnp.ones((ATT_VROWS - LANES, ATT_TK), BF16)
    u_ref[...] = seg(c_u, RW_COLS)


def _in_proj(h, w_in, positions, gate_bias, kv, *, batch, seq, tm):
    t, d = h.shape
    cols = w_in.shape[1]
    nst = seq // tm
    mem_len = kv.shape[0] // batch
    inv_freq = jnp.power(jnp.float32(ROPE_THETA),
                         -jnp.arange(ROT_HALF, dtype=F32) * (2.0 / ROT_DIM)).reshape(ROT_HALF, 1)
    row = lambda w: pl.BlockSpec((tm, w), lambda i: (i, 0))
    sd = jax.ShapeDtypeStruct
    out_shape = (sd((batch * DA_WIDTH, seq), BF16), sd((t, DA_WIDTH), BF16),
                 sd((batch * DA_HEADS, seq // ATT_TK, ATT_VROWS, ATT_TK), BF16),
                 sd((t, RW_COLS), F32), sd((t, XA_WIDTH), BF16), sd((t, 3 * d), BF16))
    out_specs = (pl.BlockSpec((DA_WIDTH, tm), lambda i: (i // nst, i % nst)),
                 row(DA_WIDTH),
                 pl.BlockSpec((DA_HEADS, tm // ATT_TK, ATT_VROWS, ATT_TK),
                              lambda i: (i // nst, i % nst, 0, 0)),
                 row(RW_COLS), row(XA_WIDTH), row(3 * d))
    return pl.pallas_call(
        functools.partial(_in_proj_kernel, d_model=d),
        grid=(t // tm,),
        in_specs=[row(d), _resident((d, cols)),
                  pl.BlockSpec((None, 1, tm), lambda i: (i, 0, 0)), _resident((ROT_HALF, 1)),
                  _resident((1, 3 * d)),
                  pl.BlockSpec((mem_len, 2 * XA_WIDTH), lambda i: (i // nst, 0))],
        out_specs=out_specs,
        out_shape=out_shape,
        compiler_params=_cparams("parallel"),
        name="in_proj",
    )(h, w_in.astype(BF16), positions.reshape(t // tm, 1, tm), inv_freq,
      gate_bias.reshape(1, 3 * d), kv)


def _lambda_kernel(q1_ref, k1_ref, q2_ref, k2_ref, out_ref, *, lam_init):
    l1 = jnp.exp(jnp.sum(q1_ref[...] * k1_ref[...], axis=-1, keepdims=True))
    l2 = jnp.exp(jnp.sum(q2_ref[...] * k2_ref[...], axis=-1, keepdims=True))
    out_ref[...] = jnp.broadcast_to(l1 - l2 + lam_init, out_ref.shape)


def _lambda(q1, k1, q2, k2, lam_init):
    vec = lambda a: a.reshape(1, DA_DH)
    out = pl.pallas_call(
        functools.partial(_lambda_kernel, lam_init=lam_init),
        out_shape=jax.ShapeDtypeStruct((8, LANES), F32),
        name="da_lambda",
    )(vec(q1), vec(k1), vec(q2), vec(k2))
    return out[0, :1]


def _diff_attn_kernel(lam_ref, q_ref, k_ref, v_ref, gain_ref, o_ref,
                      qm_s, m_s, acc_s, *, tq, out_scale):
    i = pl.program_id(2)
    lam = lam_ref[0]
    tk = ATT_TK
    ns = tq // tk
    q = q_ref[...]
    first = lax.broadcasted_iota(jnp.int32, (LANES, 1), 0) < DA_DH
    zero = jnp.zeros_like(q)
    qm_s[0] = jnp.where(first, q, zero)
    qm_s[1] = jnp.where(first, zero, q)
    m_s[...] = jnp.full(m_s.shape, NEG_BIG, F32)
    acc_s[...] = jnp.zeros(acc_s.shape, F32)
    key_c = lax.broadcasted_iota(jnp.int32, (tk, tk), 0) // CHUNK
    qry_c = lax.broadcasted_iota(jnp.int32, (tk, tk), 1) // CHUNK
    causal = key_c <= qry_c

    def scores(unit):
        j, c, mi, _ = unit
        k = k_ref[pl.ds(pl.multiple_of(j * tk, tk), tk), :]
        return jnp.dot(k, qm_s[mi, :, c * tk:(c + 1) * tk], preferred_element_type=F32)

    def absorb(unit, s):
        j, c, mi, masked = unit
        cs = slice(c * tk, (c + 1) * tk)
        if masked:
            s = jnp.where(causal, s, NEG_BIG)
        m_old = m_s[mi, :, cs]
        m_ref = jnp.maximum(m_old, jnp.max(s, axis=0, keepdims=True)).astype(BF16)
        m_new = m_ref.astype(F32)
        alpha = jnp.exp2(m_old - m_new)
        p = jnp.exp2(s.astype(BF16) - m_ref)
        m_s[mi, :, cs] = m_new
        acc_s[mi, :, cs] = alpha * acc_s[mi, :, cs] + jnp.dot(
            v_ref[j], p, preferred_element_type=F32)

    def run(units):
        pending = []
        for idx in range(len(units) + ATT_LOOKAHEAD):
            if idx < len(units):
                pending.append(scores(units[idx]))
            if idx >= ATT_LOOKAHEAD:
                absorb(units[idx - ATT_LOOKAHEAD], pending[idx - ATT_LOOKAHEAD])

    def body(jj, carry):
        run([(jj * ATT_TILES_PER_STEP + t, c, mi, False)
             for t in range(ATT_TILES_PER_STEP) for c in range(ns) for mi in range(2)])
        return carry

    lax.fori_loop(0, i * ns // ATT_TILES_PER_STEP, body, 0)
    run([(i * ns + d, c, mi, c == d)
         for d in range(ns) for c in range(d, ns) for mi in range(2)])
    o = (acc_s[0, 0:LANES] / acc_s[0, LANES:LANES + 1]
         - lam * (acc_s[1, 0:LANES] / acc_s[1, LANES:LANES + 1]))
    o = o * lax.rsqrt(jnp.mean(o * o, axis=0, keepdims=True) + EPS) * gain_ref[...]
    o_ref[...] = (o * out_scale).T.astype(BF16)


def _diff_attn(lam, q_t, k, v_t, gain, *, batch, seq, tq, out_scale):
    nq = seq // tq
    nk = seq // ATT_TK
    assert (tq // ATT_TK) % ATT_TILES_PER_STEP == 0
    return pl.pallas_call(
        functools.partial(_diff_attn_kernel, tq=tq, out_scale=out_scale),
        grid=(batch, DA_HEADS, nq),
        scratch_shapes=[pltpu.VMEM((2, LANES, tq), BF16), pltpu.VMEM((2, 1, tq), F32),
                        pltpu.VMEM((2, ATT_VROWS, tq), F32)],
        in_specs=[pl.BlockSpec(memory_space=pltpu.SMEM),
                  pl.BlockSpec((LANES, tq), lambda b, h, i: (b * DA_HEADS + h, i)),
                  pl.BlockSpec((seq, LANES), lambda b, h, i: (b, h)),
                  pl.BlockSpec((None, nk, ATT_VROWS, ATT_TK),
                               lambda b, h, i: (b * DA_HEADS + h, 0, 0, 0)),
                  _resident((LANES, 1))],
        out_specs=pl.BlockSpec((tq, LANES), lambda b, h, i: (b * nq + i, h)),
        out_shape=jax.ShapeDtypeStruct(k.shape, BF16),
        compiler_params=_cparams("parallel", "parallel", "arbitrary"),
        name="diff_attn",
    )(lam, q_t, k, v_t, gain.reshape(LANES, 1))


def _stack_heads(x):
    zero = jnp.zeros_like(x)
    return jnp.concatenate([jnp.where(_head_mask(h), x, zero) for h in range(RW_HEADS)], axis=0)


def _rwkv_kernel(u_ref, mu_ref, wlr_ref, wab_ref, g2_ref, kk_ref, ka_ref, rk_ref,
                 lnw_ref, lnb_ref, o_ref,
                 carry_ref, state_ref, y_s, ar_s, arb_s, tcat_s, av_s, bkh_s, dec_s, *, ts):
    c_len = RW_CHUNK
    w = RW_WIDTH
    n4 = RW_HEADS * c_len
    n_chunks = ts // c_len
    groups = [range(g0, min(g0 + RW_GROUP, n_chunks)) for g0 in range(0, n_chunks, RW_GROUP)]

    @pl.when(pl.program_id(1) == 0)
    def _():
        carry_ref[...] = jnp.zeros_like(carry_ref)
        state_ref[...] = jnp.zeros_like(state_ref)

    hr = lax.broadcasted_iota(jnp.int32, (w, w), 0) // RW_DH
    hc = lax.broadcasted_iota(jnp.int32, (w, w), 1) // RW_DH
    head_ones = (hr == hc).astype(BF16)
    tr = lax.broadcasted_iota(jnp.int32, (c_len, c_len), 0)
    tc = lax.broadcasted_iota(jnp.int32, (c_len, c_len), 1)
    tri = (tc <= tr).astype(BF16)
    tt = lax.broadcasted_iota(jnp.int32, (c_len, n4), 0)
    ss = lax.broadcasted_iota(jnp.int32, (c_len, n4), 1) % c_len
    strict = ss < tt
    incl = ss <= tt
    br = lax.broadcasted_iota(jnp.int32, (n4, n4), 0)
    bc = lax.broadcasted_iota(jnp.int32, (n4, n4), 1)
    blockdiag = (br // c_len) == (bc // c_len)
    eye = (br == bc).astype(F32)
    levels = int(math.log2(c_len)) - 1
    lane = lax.broadcasted_iota(jnp.int32, (1, LANES), 1)

    def rowblock_sum(x):
        return sum(x[h * c_len:(h + 1) * c_len, :] for h in range(RW_HEADS))

    tok = {}
    xs = {}

    def shift_and_heads(q, group):
        lo, hi = group[0] * c_len, (group[-1] + 1) * c_len
        n = hi - lo
        u = u_ref[q, lo:hi, :]
        before = carry_ref[q] if lo == 0 else u_ref[q, lo - 1:lo, :]
        row = lax.broadcasted_iota(jnp.int32, (n, 1), 0)
        prev = jnp.where(row == 0, before, pltpu.roll(u, 1, axis=0))
        us = u + (prev - u) * mu_ref[...]
        r = us[:, 0:w]
        k = us[:, w:2 * w]
        v = us[:, 2 * w:3 * w]
        low = us[:, 3 * w:3 * w + LANES]
        low = jnp.where(lane < RW_DECAY_RANK, jnp.tanh(low), low)
        lr = _mm(low, wlr_ref[...])
        g = _mm(jax.nn.sigmoid(us[:, 3 * w + LANES:3 * w + LANES + RW_GATE_RANK]), g2_ref[...])
        kk = k * kk_ref[...]
        ksq = _mm(kk * kk, head_ones)
        yield
        z = -(wab_ref[:, 0:w] + lr[:, 0:w])
        softplus = jnp.maximum(z, 0.0) + jnp.log(1.0 + jnp.exp(-jnp.abs(z)))
        lw = -jnp.exp(-softplus - 0.5)
        a = jax.nn.sigmoid(wab_ref[:, w:2 * w] + lr[:, w:2 * w])
        kk = kk / jnp.maximum(jnp.sqrt(ksq), 1e-12)
        k2 = k * (1.0 + (a - 1.0) * ka_ref[...])
        for i, c in enumerate(group):
            sl = slice(i * c_len, (i + 1) * c_len)
            tok[q, c] = dict(r=r[sl], k2=k2[sl], v=v[sl], lw=lw[sl], kk=kk[sl],
                             bv=kk[sl] * a[sl], g=g[sl])
        yield

    def front(q, c):
        t = tok[q, c]
        cum_c = _mm_f32_left(tri, t["lw"], RW_CUM_PASSES)
        yield
        last = cum_c[c_len - 1:c_len, :]
        inv = jnp.exp(-cum_c)
        tail = jnp.exp(last - cum_c)
        a_t = -t["kk"] * jnp.exp(cum_c - t["lw"])
        r_t = t["r"] * jnp.exp(cum_c)
        ar = jnp.concatenate([a_t, r_t], axis=0).astype(BF16)
        ar_s[q, c] = ar
        bk = jnp.concatenate([_stack_heads(t["bv"] * inv), _stack_heads(t["k2"] * inv)], axis=0)
        gram = _mm_nt(ar, bk)
        yield
        zero = jnp.zeros((c_len, n4), F32)
        a_ab = jnp.where(strict, gram[:c_len, :n4], zero)
        a_ak = jnp.where(strict, gram[:c_len, n4:], zero)
        a_rb = jnp.where(incl, gram[c_len:, :n4], zero)
        a_rk = jnp.where(incl, gram[c_len:, n4:], zero)
        arb_s[q, c] = a_rb.astype(BF16)
        xs[q, c] = jnp.where(blockdiag, jnp.concatenate([a_ab] * RW_HEADS, axis=0), 0.0)
        av_s[q, c] = _mm(jnp.concatenate([a_ak, a_rk], axis=0), _stack_heads(t["v"]))
        yield
        bkh = jnp.concatenate([t["bv"] * tail, t["k2"] * tail], axis=0)
        bkh_s[q, c] = bkh.T.astype(BF16)
        dec_s[q, c] = jnp.broadcast_to(jnp.exp(last), (LANES, w)).T
        yield

    def invert(q, c):
        x = xs.pop((q, c))
        p_ = eye + x
        y_ = _mm(x, x)
        yield
        for lvl in range(levels):
            if lvl < levels - 1:
                z_ = _mm(jnp.concatenate([p_, y_], axis=0), y_)
                p_ = p_ + z_[:n4, :]
                y_ = z_[n4:, :]
            else:
                p_ = p_ + _mm(p_, y_)
            yield
        tcat_s[q, c] = rowblock_sum(p_).astype(BF16)

    def sequential(q, chunks):
        state = states[q]
        for c in chunks:
            sl = slice(c * c_len, (c + 1) * c_len)
            av = av_s[q, c]
            xy = _mm(ar_s[q, c], state)
            yield
            z_c = xy[:c_len, :] + av[:c_len, :]
            u_c = _mm(tcat_s[q, c], _stack_heads(z_c))
            yield
            y_s[q, sl, :] = xy[c_len:, :] + av[c_len:, :] + _mm(arb_s[q, c], _stack_heads(u_c))
            upd = _mm(bkh_s[q, c], jnp.concatenate([u_c, tok[q, c]["v"]], axis=0))
            state = state * dec_s[q, c][:, 0:1] + jnp.where(blockdiag, upd, 0.0)
            yield
        states[q] = state

    def finish(q, group):
        lo, hi = group[0] * c_len, (group[-1] + 1) * c_len
        cat = lambda name: jnp.concatenate([tok[q, c][name] for c in group], axis=0)
        r, k2, v, g = cat("r"), cat("k2"), cat("v"), cat("g")
        y = y_s[q, lo:hi, :]
        mean = _mm(y, head_ones) * (1.0 / RW_DH)
        bonus = _mm(r * k2 * rk_ref[...], head_ones) * v
        yield
        yc = y - mean
        var = _mm(yc * yc, head_ones) * (1.0 / RW_DH)
        yield
        yn = yc * lax.rsqrt(var + RW_GN_EPS) * lnw_ref[...] + lnb_ref[...]
        o_ref[q, lo:hi, :] = ((yn + bonus) * g).astype(BF16)
        yield

    def emit(activities):
        live = list(activities)
        while live:
            for act in list(live):
                if next(act, StopIteration) is StopIteration:
                    live.remove(act)

    seqs = range(u_ref.shape[0])
    states = [state_ref[q] for q in seqs]
    n_g = len(groups)
    for step in range(n_g + 4):
        acts = []
        for q in seqs:
            if step < n_g:
                acts.append(shift_and_heads(q, groups[step]))
            if 0 <= step - 1 < n_g:
                acts += [front(q, c) for c in groups[step - 1]]
            if 0 <= step - 2 < n_g:
                acts += [invert(q, c) for c in groups[step - 2]]
            if 0 <= step - 3 < n_g:
                acts.append(sequential(q, groups[step - 3]))
            if 0 <= step - 4 < n_g:
                acts.append(finish(q, groups[step - 4]))
        emit(acts)
    for q in seqs:
        state_ref[q] = states[q]
        carry_ref[q] = u_ref[q, ts - 1:ts, :]


def _rwkv(u, mu, w0, w2, a0, a2, g2, k_k, k_a, r_k, ln_w, ln_b, *, batch, seq, ts):
    w = RW_WIDTH
    ns = seq // ts
    nq = RW_SEQS if batch % RW_SEQS == 0 else 1
    zeros = lambda r, c: jnp.zeros((r, c), F32)
    w_lr = jnp.concatenate([jnp.concatenate([w2, zeros(RW_DECAY_RANK, w)], axis=1),
                            jnp.concatenate([zeros(RW_A_RANK, w), a2], axis=1)], axis=0)
    w_ab = jnp.concatenate([w0, a0]).reshape(1, 2 * w)
    vec = lambda p: p.reshape(1, w)
    nc = ts // RW_CHUNK
    c2 = 2 * RW_CHUNK
    scratch = [pltpu.VMEM((nq, 1, RW_COLS), F32), pltpu.VMEM((nq, w, w), F32),
               pltpu.VMEM((nq, ts, w), F32),
               pltpu.VMEM((nq, nc, c2, w), BF16), pltpu.VMEM((nq, nc, RW_CHUNK, w), BF16),
               pltpu.VMEM((nq, nc, RW_CHUNK, w), BF16), pltpu.VMEM((nq, nc, c2, w), F32),
               pltpu.VMEM((nq, nc, w, c2), BF16), pltpu.VMEM((nq, nc, w, LANES), F32)]
    out = pl.pallas_call(
        functools.partial(_rwkv_kernel, ts=ts),
        grid=(batch // nq, ns),
        in_specs=[pl.BlockSpec((nq, ts, RW_COLS), lambda b, i: (b, i, 0)),
                  _resident((1, RW_COLS)), _resident((LANES, 2 * w)), _resident((1, 2 * w)),
                  _resident((RW_GATE_RANK, w))] + [_resident((1, w))] * 5,
        out_specs=pl.BlockSpec((nq, ts, w), lambda b, i: (b, i, 0)),
        out_shape=jax.ShapeDtypeStruct((batch, seq, w), BF16),
        scratch_shapes=scratch,
        compiler_params=_cparams("parallel", "arbitrary"),
        name="rwkv7",
    )(u.reshape(batch, seq, RW_COLS), mu.reshape(1, RW_COLS), w_lr.astype(BF16), w_ab,
      g2.astype(BF16), vec(k_k), vec(k_a), vec(r_k), vec(ln_w), vec(ln_b))
    return out.reshape(batch * seq, w)


def _mem_kv_kernel(mem_ref, g_ref, w_ref, kv_ref):
    h = _rms(mem_ref[...], g_ref[...]).astype(BF16)
    kv_ref[...] = jnp.dot(h, w_ref[...], preferred_element_type=F32).astype(BF16)


def _mem_kv(mem, gain, w_kv, *, batch, mem_len):
    d = mem.shape[-1]
    n = w_kv.shape[1]
    return pl.pallas_call(
        _mem_kv_kernel,
        grid=(batch,),
        in_specs=[pl.BlockSpec((mem_len, d), lambda b: (b, 0)), _resident((1, d)),
                  _resident((d, n))],
        out_specs=pl.BlockSpec((mem_len, n), lambda b: (b, 0)),
        out_shape=jax.ShapeDtypeStruct((batch * mem_len, n), BF16),
        compiler_params=_cparams("parallel"),
        name="mem_kv",
    )(mem.reshape(batch * mem_len, d), gain.reshape(1, d), w_kv.astype(BF16))


def _pick(n, pref):
    t = min(n, pref)
    while n % t:
        t //= 2
    return t


def kernel(x, mem, positions, ffn1_norm, ffn1_w1, ffn1_w3, ffn1_w2, mix_norm, w_in, gate_bias, da_lambda_q1, da_lambda_k1, da_lambda_q2, da_lambda_k2, da_subln, rw_mu, rw_w0, rw_w2, rw_a0, rw_a2, rw_g2, rw_k_k, rw_k_a, rw_r_k, rw_ln_w, rw_ln_b, mem_norm, w_mem_kv, w_up_a, w_up_b, w_up_c, w_out, ffn2_norm, ffn2_w1, ffn2_w3, ffn2_w2, final_norm):
    batch, seq, d = x.shape
    mem_len = mem.shape[1]
    depth = ffn1_norm.shape[0]
    t = batch * seq
    tm = _pick(seq, 512)
    tq = _pick(seq, 2048)
    ts = _pick(seq, 512)

    xf = x.reshape(t, d)
    for l in range(depth):
        lam_init = 0.8 - 0.6 * math.exp(-0.3 * l)
        x1, h = _ffn_first(xf, ffn1_norm[l], ffn1_w1[l], ffn1_w3[l], ffn1_w2[l], mix_norm[l],
                           tm=tm)
        kv = _mem_kv(mem, mem_norm[l], w_mem_kv[l], batch=batch, mem_len=mem_len)
        q, k, v, u, o_c, gates = _in_proj(h, w_in[l], positions, gate_bias[l], kv,
                                          batch=batch, seq=seq, tm=tm)
        lam = _lambda(da_lambda_q1[l], da_lambda_k1[l], da_lambda_q2[l], da_lambda_k2[l], lam_init)
        o_a = _diff_attn(lam, q, k, v, da_subln[l], batch=batch, seq=seq, tq=tq,
                         out_scale=1.0 - lam_init)
        o_b = _rwkv(u, rw_mu[l], rw_w0[l], rw_w2[l], rw_a0[l], rw_a2[l], rw_g2[l], rw_k_k[l],
                    rw_k_a[l], rw_r_k[l], rw_ln_w[l], rw_ln_b[l], batch=batch, seq=seq, ts=ts)
        assert depth == 1, "stacked layers need the un-normalised output of the second half step"
        xf = _merge_ffn(x1, o_a, o_b, o_c, gates, w_up_a[l], w_up_b[l], w_up_c[l], w_out[l],
                        ffn2_norm[l], ffn2_w1[l], ffn2_w3[l], ffn2_w2[l], final_norm,
                        tm=tm)
    return xf.reshape(batch, seq, d)
```

```python
import functools
import math

import jax
import jax.numpy as jnp
from jax import lax
from jax.experimental import pallas as pl
from jax.experimental.pallas import tpu as pltpu

F32 = jnp.float32
BF16 = jnp.bfloat16

CHUNK = 64
DA_HEADS = 4
DA_DH = 64
DA_WIDTH = DA_HEADS * 2 * DA_DH
RW_HEADS = 4
RW_DH = 64
RW_WIDTH = RW_HEADS * RW_DH
RW_DECAY_RANK = 64
RW_A_RANK = 64
RW_GATE_RANK = 128
RW_COLS = 3 * RW_WIDTH + RW_DECAY_RANK + RW_A_RANK + RW_GATE_RANK
RW_CHUNK = 64
RW_CUM_PASSES = 2
RW_SEQS = 2
RW_GROUP = 2
XA_HEADS = 4
XA_DH = 64
XA_WIDTH = XA_HEADS * XA_DH
ROPE_THETA = 500000.0
ROT_DIM = DA_DH // 4
ROT_HALF = ROT_DIM // 2
EPS = 1e-6
RW_GN_EPS = 64e-5
LANES = 128
NEG_BIG = -1e30
ATT_TK = 256
ATT_TILES_PER_STEP = 4
ATT_VROWS = LANES + 16
ATT_LOOKAHEAD = 4
LOG2E = math.log2(math.e)
FFN_ROW_PARTS = 2
VMEM_LIMIT = 56 * 1024 * 1024


def _cparams(*sem, flags=None):
    return pltpu.CompilerParams(dimension_semantics=sem, vmem_limit_bytes=VMEM_LIMIT, flags=flags)


def _resident(shape):
    return pl.BlockSpec(shape, lambda *_: (0,) * len(shape), pipeline_mode=pl.Buffered(1))


def _mm(a, b):
    return jnp.dot(a.astype(BF16), b.astype(BF16), preferred_element_type=F32)


def _mm_nt(a, b):
    return lax.dot_general(a.astype(BF16), b.astype(BF16), (((1,), (1,)), ((), ())),
                           preferred_element_type=F32)


def _split_bf16(a, passes):
    parts = []
    for _ in range(passes):
        part = a.astype(BF16)
        parts.append(part)
        a = a - part.astype(F32)
    return parts


def _mm_f32_left(m_bf16, a, passes):
    return sum(jnp.dot(m_bf16, p, preferred_element_type=F32) for p in _split_bf16(a, passes))


def _head_mask(h, width=RW_WIDTH):
    lane = lax.broadcasted_iota(jnp.int32, (1, width), 1)
    return (lane // RW_DH) == h


def _rms(x, gain):
    return x * lax.rsqrt(jnp.mean(x * x, axis=-1, keepdims=True) + EPS) * gain


def _swiglu_half_step(x, gin, w1_ref, w3_ref, w2_ref):
    rows = x.shape[0]
    parts = FFN_ROW_PARTS if rows % (FFN_ROW_PARTS * 16) == 0 else 1
    n = rows // parts
    hs = [_rms(x[p * n:(p + 1) * n], gin).astype(BF16) for p in range(parts)]
    pre = [(jnp.dot(h, w1_ref[...], preferred_element_type=F32),
            jnp.dot(h, w3_ref[...], preferred_element_type=F32)) for h in hs]
    outs = []
    for p, (a, b) in enumerate(pre):
        act = (a * jax.nn.sigmoid(a) * b).astype(BF16)
        outs.append(x[p * n:(p + 1) * n] + 0.5 * jnp.dot(act, w2_ref[...],
                                                         preferred_element_type=F32))
    return jnp.concatenate(outs, axis=0)


def _ffn_first_kernel(x_ref, gin_ref, w1_ref, w3_ref, w2_ref, gout_ref, y_ref, h_ref):
    y = _swiglu_half_step(x_ref[...], gin_ref[...], w1_ref, w3_ref, w2_ref)
    y_ref[...] = y
    h_ref[...] = _rms(y, gout_ref[...]).astype(BF16)


def _ffn_first(x, gin, w1, w3, w2, gout, *, tm):
    t, d = x.shape
    d_ff = w1.shape[1]
    row = pl.BlockSpec((tm, d), lambda i: (i, 0))
    return pl.pallas_call(
        _ffn_first_kernel,
        grid=(t // tm,),
        in_specs=[row, _resident((1, d)), _resident((d, d_ff)), _resident((d, d_ff)),
                  _resident((d_ff, d)), _resident((1, d))],
        out_specs=(row, row),
        out_shape=(jax.ShapeDtypeStruct((t, d), F32), jax.ShapeDtypeStruct((t, d), BF16)),
        compiler_params=_cparams("parallel"),
        name="ffn_first",
    )(x, gin.reshape(1, d), w1.astype(BF16), w3.astype(BF16), w2.astype(BF16), gout.reshape(1, d))


def _merge_ffn_kernel(x_ref, oa_ref, ob_ref, oc_ref, g_ref, wa_ref, wb_ref, wc_ref, wo_ref,
                      gin_ref, w1_ref, w3_ref, w2_ref, gout_ref, o_ref):
    d = x_ref.shape[1]
    dot = functools.partial(jnp.dot, preferred_element_type=F32)
    merged = (g_ref[:, 0:d].astype(F32) * dot(oa_ref[...], wa_ref[...])
              + g_ref[:, d:2 * d].astype(F32) * dot(ob_ref[...], wb_ref[...])
              + g_ref[:, 2 * d:3 * d].astype(F32) * dot(oc_ref[...], wc_ref[...]))
    x2 = x_ref[...] + dot(merged.astype(BF16), wo_ref[...])
    y = _swiglu_half_step(x2, gin_ref[...], w1_ref, w3_ref, w2_ref)
    o_ref[...] = _rms(y, gout_ref[...])


def _merge_ffn(x, oa, ob, oc, gates, wa, wb, wc, wo, gin, w1, w3, w2, gout, *, tm):
    t, d = x.shape
    d_ff = w1.shape[1]
    row = lambda w: pl.BlockSpec((tm, w), lambda i: (i, 0))
    bf = lambda a: a.astype(BF16)
    return pl.pallas_call(
        _merge_ffn_kernel,
        grid=(t // tm,),
        in_specs=[row(d), row(DA_WIDTH), row(RW_WIDTH), row(XA_WIDTH), row(3 * d),
                  _resident(wa.shape), _resident(wb.shape), _resident(wc.shape),
                  _resident(wo.shape), _resident((1, d)), _resident((d, d_ff)),
                  _resident((d, d_ff)), _resident((d_ff, d)), _resident((1, d))],
        out_specs=row(d),
        out_shape=jax.ShapeDtypeStruct((t, d), F32),
        compiler_params=_cparams("parallel"),
        name="merge_ffn",
    )(x, oa, ob, oc, gates, bf(wa), bf(wb), bf(wc), bf(wo), gin.reshape(1, d), bf(w1), bf(w3),
      bf(w2), gout.reshape(1, d))


def _rope_tables(pos, inv_freq):
    ang = inv_freq * pos.astype(F32)
    c = jnp.cos(ang)
    s = jnp.sin(ang)
    one = jnp.ones_like(c)
    zero = jnp.zeros_like(c)
    pad = DA_DH // ROT_HALF - 2

    def lanes(first, second, fill):
        head = [first, second] + [fill] * pad
        return jnp.concatenate(head * (LANES // DA_DH), axis=0).T

    return lanes(c, c, one), lanes(-s, zero, zero), lanes(zero, s, zero)


def _mem_attention(q, kv_ref):
    k = kv_ref[:, 0:XA_WIDTH]
    v = kv_ref[:, XA_WIDTH:2 * XA_WIDTH]
    zero = jnp.zeros_like(q)
    out = jnp.zeros(q.shape, F32)
    for h in range(XA_HEADS):
        hm = _head_mask(h, XA_WIDTH)
        s = _mm_nt(jnp.where(hm, q, zero), k)
        p = jnp.exp(s - jnp.max(s, axis=-1, keepdims=True))
        o = jnp.dot(p.astype(BF16), v, preferred_element_type=F32)
        out = jnp.where(hm, o / jnp.sum(p, axis=-1, keepdims=True), out)
    return out


def _in_proj_kernel(h_ref, w_ref, pos_ref, invf_ref, gb_ref, kv_ref,
                    q_ref, k_ref, v_ref, u_ref, oc_ref, g_ref, *, d_model):
    h = h_ref[...]
    cos_t, sin_a, sin_b = _rope_tables(pos_ref[...], invf_ref[...])

    def seg(lo, width):
        return jnp.dot(h, w_ref[:, lo:lo + width], preferred_element_type=F32)

    def rope(t):
        nxt = pltpu.roll(t, LANES - ROT_HALF, axis=1)
        prv = pltpu.roll(t, ROT_HALF, axis=1)
        return t * cos_t + nxt * sin_a + prv * sin_b

    tm = h.shape[0]
    pair = 2 * LANES
    c_q, c_k, c_v = 0, DA_WIDTH, 2 * DA_WIDTH
    c_u = 3 * DA_WIDTH
    c_x = c_u + RW_COLS
    c_g = c_x + XA_WIDTH
    for j in range(3):
        gl = seg(c_g + j * d_model, d_model) + gb_ref[:, j * d_model:(j + 1) * d_model]
        g_ref[:, j * d_model:(j + 1) * d_model] = jax.nn.sigmoid(gl).astype(BF16)
    qx = (seg(c_x, XA_WIDTH) * (XA_DH ** -0.5)).astype(BF16)
    oc_ref[...] = _mem_attention(qx, kv_ref).astype(BF16)
    for j in range(DA_WIDTH // pair):
        t2 = seg(c_q + j * pair, pair)
        for e in range(2):
            t = rope(t2[:, e * LANES:(e + 1) * LANES]) * (DA_DH ** -0.5 * LOG2E)
            q_ref[(2 * j + e) * LANES:(2 * j + e + 1) * LANES, :] = t.T.astype(BF16)
    for j in range(DA_WIDTH // pair):
        t2 = seg(c_k + j * pair, pair)
        for e in range(2):
            t = rope(t2[:, e * LANES:(e + 1) * LANES])
            k_ref[:, (2 * j + e) * LANES:(2 * j + e + 1) * LANES] = t.astype(BF16)
    for j in range(DA_WIDTH // pair):
        t2 = seg(c_v + j * pair, pair)
        for e in range(2):
            for jj in range(tm // ATT_TK):
                v_ref[2 * j + e, jj, 0:LANES, :] = t2[jj * ATT_TK:(jj + 1) * ATT_TK,
                                                      e * LANES:(e + 1) * LANES].T.astype(BF16)
                v_ref[2 * j + e, jj, LANES:ATT_VROWS, :] = jnp.ones((ATT_VROWS - LANES, ATT_TK), BF16)
    u_ref[...] = seg(c_u, RW_COLS)


def _in_proj(h, w_in, positions, gate_bias, kv, *, batch, seq, tm):
    t, d = h.shape
    cols = w_in.shape[1]
    nst = seq // tm
    mem_len = kv.shape[0] // batch
    inv_freq = jnp.power(jnp.float32(ROPE_THETA),
                         -jnp.arange(ROT_HALF, dtype=F32) * (2.0 / ROT_DIM)).reshape(ROT_HALF, 1)
    row = lambda w: pl.BlockSpec((tm, w), lambda i: (i, 0))
    sd = jax.ShapeDtypeStruct
    out_shape = (sd((batch * DA_WIDTH, seq), BF16), sd((t, DA_WIDTH), BF16),
                 sd((batch * DA_HEADS, seq // ATT_TK, ATT_VROWS, ATT_TK), BF16),
                 sd((t, RW_COLS), F32), sd((t, XA_WIDTH), BF16), sd((t, 3 * d), BF16))
    out_specs = (pl.BlockSpec((DA_WIDTH, tm), lambda i: (i // nst, i % nst)),
                 row(DA_WIDTH),
                 pl.BlockSpec((DA_HEADS, tm // ATT_TK, ATT_VROWS, ATT_TK),
                              lambda i: (i // nst, i % nst, 0, 0)),
                 row(RW_COLS), row(XA_WIDTH), row(3 * d))
    return pl.pallas_call(
        functools.partial(_in_proj_kernel, d_model=d),
        grid=(t // tm,),
        in_specs=[row(d), _resident((d, cols)),
                  pl.BlockSpec((None, 1, tm), lambda i: (i, 0, 0)), _resident((ROT_HALF, 1)),
                  _resident((1, 3 * d)),
                  pl.BlockSpec((mem_len, 2 * XA_WIDTH), lambda i: (i // nst, 0))],
        out_specs=out_specs,
        out_shape=out_shape,
        compiler_params=_cparams("parallel"),
        name="in_proj",
    )(h, w_in.astype(BF16), positions.reshape(t // tm, 1, tm), inv_freq,
      gate_bias.reshape(1, 3 * d), kv)


def _lambda_kernel(q1_ref, k1_ref, q2_ref, k2_ref, out_ref, *, lam_init):
    l1 = jnp.exp(jnp.sum(q1_ref[...] * k1_ref[...], axis=-1, keepdims=True))
    l2 = jnp.exp(jnp.sum(q2_ref[...] * k2_ref[...], axis=-1, keepdims=True))
    out_ref[...] = jnp.broadcast_to(l1 - l2 + lam_init, out_ref.shape)


def _lambda(q1, k1, q2, k2, lam_init):
    vec = lambda a: a.reshape(1, DA_DH)
    out = pl.pallas_call(
        functools.partial(_lambda_kernel, lam_init=lam_init),
        out_shape=jax.ShapeDtypeStruct((8, LANES), F32),
        name="da_lambda",
    )(vec(q1), vec(k1), vec(q2), vec(k2))
    return out[0, :1]


def _diff_attn_kernel(lam_ref, q_ref, k_ref, v_ref, gain_ref, o_ref,
                      qm_s, m_s, acc_s, *, tq, out_scale):
    i = pl.program_id(2)
    lam = lam_ref[0]
    tk = ATT_TK
    ns = tq // tk
    q = q_ref[...]
    first = lax.broadcasted_iota(jnp.int32, (LANES, 1), 0) < DA_DH
    zero = jnp.zeros_like(q)
    qm_s[0] = jnp.where(first, q, zero)
    qm_s[1] = jnp.where(first, zero, q)
    m_s[...] = jnp.full(m_s.shape, NEG_BIG, F32)
    acc_s[...] = jnp.zeros(acc_s.shape, F32)
    key_c = lax.broadcasted_iota(jnp.int32, (tk, tk), 0) // CHUNK
    qry_c = lax.broadcasted_iota(jnp.int32, (tk, tk), 1) // CHUNK
    causal = key_c <= qry_c

    def scores(unit):
        j, c, mi, _ = unit
        k = k_ref[pl.ds(pl.multiple_of(j * tk, tk), tk), :]
        return jnp.dot(k, qm_s[mi, :, c * tk:(c + 1) * tk], preferred_element_type=F32)

    def absorb(unit, s):
        j, c, mi, masked = unit
        cs = slice(c * tk, (c + 1) * tk)
        if masked:
            s = jnp.where(causal, s, NEG_BIG)
        m_old = m_s[mi, :, cs]
        m_new = jnp.maximum(m_old, jnp.max(s, axis=0, keepdims=True))
        alpha = jnp.exp2(m_old - m_new)
        p = jnp.exp2((s - m_new).astype(BF16))
        m_s[mi, :, cs] = m_new
        acc_s[mi, :, cs] = alpha * acc_s[mi, :, cs] + jnp.dot(
            v_ref[j], p, preferred_element_type=F32)

    def run(units):
        pending = []
        for idx in range(len(units) + ATT_LOOKAHEAD):
            if idx < len(units):
                pending.append(scores(units[idx]))
            if idx >= ATT_LOOKAHEAD:
                absorb(units[idx - ATT_LOOKAHEAD], pending[idx - ATT_LOOKAHEAD])

    def body(jj, carry):
        run([(jj * ATT_TILES_PER_STEP + t, c, mi, False)
             for t in range(ATT_TILES_PER_STEP) for c in range(ns) for mi in range(2)])
        return carry

    lax.fori_loop(0, i * ns // ATT_TILES_PER_STEP, body, 0)
    run([(i * ns + d, c, mi, c == d)
         for d in range(ns) for c in range(d, ns) for mi in range(2)])
    o = (acc_s[0, 0:LANES] / acc_s[0, LANES:LANES + 1]
         - lam * (acc_s[1, 0:LANES] / acc_s[1, LANES:LANES + 1]))
    o = o * lax.rsqrt(jnp.mean(o * o, axis=0, keepdims=True) + EPS) * gain_ref[...]
    o_ref[...] = (o * out_scale).T.astype(BF16)


def _diff_attn(lam, q_t, k, v_t, gain, *, batch, seq, tq, out_scale):
    nq = seq // tq
    nk = seq // ATT_TK
    assert (tq // ATT_TK) % ATT_TILES_PER_STEP == 0
    return pl.pallas_call(
        functools.partial(_diff_attn_kernel, tq=tq, out_scale=out_scale),
        grid=(batch, DA_HEADS, nq),
        scratch_shapes=[pltpu.VMEM((2, LANES, tq), BF16), pltpu.VMEM((2, 1, tq), F32),
                        pltpu.VMEM((2, ATT_VROWS, tq), F32)],
        in_specs=[pl.BlockSpec(memory_space=pltpu.SMEM),
                  pl.BlockSpec((LANES, tq), lambda b, h, i: (b * DA_HEADS + h, i)),
                  pl.BlockSpec((seq, LANES), lambda b, h, i: (b, h)),
                  pl.BlockSpec((None, nk, ATT_VROWS, ATT_TK),
                               lambda b, h, i: (b * DA_HEADS + h, 0, 0, 0)),
                  _resident((LANES, 1))],
        out_specs=pl.BlockSpec((tq, LANES), lambda b, h, i: (b * nq + i, h)),
        out_shape=jax.ShapeDtypeStruct(k.shape, BF16),
        compiler_params=_cparams("parallel", "parallel", "arbitrary"),
        name="diff_attn",
    )(lam, q_t, k, v_t, gain.reshape(LANES, 1))


def _stack_heads(x):
    zero = jnp.zeros_like(x)
    return jnp.concatenate([jnp.where(_head_mask(h), x, zero) for h in range(RW_HEADS)], axis=0)


def _rwkv_kernel(u_ref, mu_ref, wlr_ref, wab_ref, g2_ref, kk_ref, ka_ref, rk_ref,
                 lnw_ref, lnb_ref, o_ref,
                 carry_ref, state_ref, y_s, ar_s, arb_s, tcat_s, av_s, bkh_s, dec_s, *, ts):
    c_len = RW_CHUNK
    w = RW_WIDTH
    n4 = RW_HEADS * c_len
    n_chunks = ts // c_len
    groups = [range(g0, min(g0 + RW_GROUP, n_chunks)) for g0 in range(0, n_chunks, RW_GROUP)]

    @pl.when(pl.program_id(1) == 0)
    def _():
        carry_ref[...] = jnp.zeros_like(carry_ref)
        state_ref[...] = jnp.zeros_like(state_ref)

    hr = lax.broadcasted_iota(jnp.int32, (w, w), 0) // RW_DH
    hc = lax.broadcasted_iota(jnp.int32, (w, w), 1) // RW_DH
    head_ones = (hr == hc).astype(BF16)
    tr = lax.broadcasted_iota(jnp.int32, (c_len, c_len), 0)
    tc = lax.broadcasted_iota(jnp.int32, (c_len, c_len), 1)
    tri = (tc <= tr).astype(BF16)
    tt = lax.broadcasted_iota(jnp.int32, (c_len, n4), 0)
    ss = lax.broadcasted_iota(jnp.int32, (c_len, n4), 1) % c_len
    strict = ss < tt
    incl = ss <= tt
    br = lax.broadcasted_iota(jnp.int32, (n4, n4), 0)
    bc = lax.broadcasted_iota(jnp.int32, (n4, n4), 1)
    blockdiag = (br // c_len) == (bc // c_len)
    eye = (br == bc).astype(F32)
    levels = int(math.log2(c_len)) - 1
    lane = lax.broadcasted_iota(jnp.int32, (1, LANES), 1)

    def rowblock_sum(x):
        return sum(x[h * c_len:(h + 1) * c_len, :] for h in range(RW_HEADS))

    tok = {}
    xs = {}

    def shift_and_heads(q, group):
        lo, hi = group[0] * c_len, (group[-1] + 1) * c_len
        n = hi - lo
        u = u_ref[q, lo:hi, :]
        before = carry_ref[q] if lo == 0 else u_ref[q, lo - 1:lo, :]
        row = lax.broadcasted_iota(jnp.int32, (n, 1), 0)
        prev = jnp.where(row == 0, before, pltpu.roll(u, 1, axis=0))
        us = u + (prev - u) * mu_ref[...]
        r = us[:, 0:w]
        k = us[:, w:2 * w]
        v = us[:, 2 * w:3 * w]
        low = us[:, 3 * w:3 * w + LANES]
        low = jnp.where(lane < RW_DECAY_RANK, jnp.tanh(low), low)
        lr = _mm(low, wlr_ref[...])
        g = _mm(jax.nn.sigmoid(us[:, 3 * w + LANES:3 * w + LANES + RW_GATE_RANK]), g2_ref[...])
        kk = k * kk_ref[...]
        ksq = _mm(kk * kk, head_ones)
        yield
        z = -(wab_ref[:, 0:w] + lr[:, 0:w])
        softplus = jnp.maximum(z, 0.0) + jnp.log(1.0 + jnp.exp(-jnp.abs(z)))
        lw = -jnp.exp(-softplus - 0.5)
        a = jax.nn.sigmoid(wab_ref[:, w:2 * w] + lr[:, w:2 * w])
        kk = kk / jnp.maximum(jnp.sqrt(ksq), 1e-12)
        k2 = k * (1.0 + (a - 1.0) * ka_ref[...])
        for i, c in enumerate(group):
            sl = slice(i * c_len, (i + 1) * c_len)
            tok[q, c] = dict(r=r[sl], k2=k2[sl], v=v[sl], lw=lw[sl], kk=kk[sl],
                             bv=kk[sl] * a[sl], g=g[sl])
        yield

    def front(q, c):
        t = tok[q, c]
        cum_c = _mm_f32_left(tri, t["lw"], RW_CUM_PASSES)
        yield
        last = cum_c[c_len - 1:c_len, :]
        inv = jnp.exp(-cum_c)
        tail = jnp.exp(last - cum_c)
        a_t = -t["kk"] * jnp.exp(cum_c - t["lw"])
        r_t = t["r"] * jnp.exp(cum_c)
        ar = jnp.concatenate([a_t, r_t], axis=0).astype(BF16)
        ar_s[q, c] = ar
        bk = jnp.concatenate([_stack_heads(t["bv"] * inv), _stack_heads(t["k2"] * inv)], axis=0)
        gram = _mm_nt(ar, bk)
        yield
        zero = jnp.zeros((c_len, n4), F32)
        a_ab = jnp.where(strict, gram[:c_len, :n4], zero)
        a_ak = jnp.where(strict, gram[:c_len, n4:], zero)
        a_rb = jnp.where(incl, gram[c_len:, :n4], zero)
        a_rk = jnp.where(incl, gram[c_len:, n4:], zero)
        arb_s[q, c] = a_rb.astype(BF16)
        xs[q, c] = jnp.where(blockdiag, jnp.concatenate([a_ab] * RW_HEADS, axis=0), 0.0)
        av_s[q, c] = _mm(jnp.concatenate([a_ak, a_rk], axis=0), _stack_heads(t["v"]))
        yield
        bkh = jnp.concatenate([t["bv"] * tail, t["k2"] * tail], axis=0)
        bkh_s[q, c] = bkh.T.astype(BF16)
        dec_s[q, c] = jnp.broadcast_to(jnp.exp(last), (LANES, w)).T
        yield

    def invert(q, c):
        x = xs.pop((q, c))
        p_ = eye + x
        y_ = _mm(x, x)
        yield
        for lvl in range(levels):
            if lvl < levels - 1:
                z_ = _mm(jnp.concatenate([p_, y_], axis=0), y_)
                p_ = p_ + z_[:n4, :]
                y_ = z_[n4:, :]
            else:
                p_ = p_ + _mm(p_, y_)
            yield
        tcat_s[q, c] = rowblock_sum(p_).astype(BF16)

    def sequential(q, chunks):
        state = states[q]
        for c in chunks:
            sl = slice(c * c_len, (c + 1) * c_len)
            av = av_s[q, c]
            xy = _mm(ar_s[q, c], state)
            yield
            z_c = xy[:c_len, :] + av[:c_len, :]
            u_c = _mm(tcat_s[q, c], _stack_heads(z_c))
            yield
            y_s[q, sl, :] = xy[c_len:, :] + av[c_len:, :] + _mm(arb_s[q, c], _stack_heads(u_c))
            upd = _mm(bkh_s[q, c], jnp.concatenate([u_c, tok[q, c]["v"]], axis=0))
            state = state * dec_s[q, c][:, 0:1] + jnp.where(blockdiag, upd, 0.0)
            yield
        states[q] = state

    def finish(q, group):
        lo, hi = group[0] * c_len, (group[-1] + 1) * c_len
        cat = lambda name: jnp.concatenate([tok[q, c][name] for c in group], axis=0)
        r, k2, v, g = cat("r"), cat("k2"), cat("v"), cat("g")
        y = y_s[q, lo:hi, :]
        mean = _mm(y, head_ones) * (1.0 / RW_DH)
        bonus = _mm(r * k2 * rk_ref[...], head_ones) * v
        yield
        yc = y - mean
        var = _mm(yc * yc, head_ones) * (1.0 / RW_DH)
        yield
        yn = yc * lax.rsqrt(var + RW_GN_EPS) * lnw_ref[...] + lnb_ref[...]
        o_ref[q, lo:hi, :] = ((yn + bonus) * g).astype(BF16)
        yield

    def emit(activities):
        live = list(activities)
        while live:
            for act in list(live):
                if next(act, StopIteration) is StopIteration:
                    live.remove(act)

    seqs = range(u_ref.shape[0])
    states = [state_ref[q] for q in seqs]
    n_g = len(groups)
    for step in range(n_g + 4):
        acts = []
        for q in seqs:
            if step < n_g:
                acts.append(shift_and_heads(q, groups[step]))
            if 0 <= step - 1 < n_g:
                acts += [front(q, c) for c in groups[step - 1]]
            if 0 <= step - 2 < n_g:
                acts += [invert(q, c) for c in groups[step - 2]]
            if 0 <= step - 3 < n_g:
                acts.append(sequential(q, groups[step - 3]))
            if 0 <= step - 4 < n_g:
                acts.append(finish(q, groups[step - 4]))
        emit(acts)
    for q in seqs:
        state_ref[q] = states[q]
        carry_ref[q] = u_ref[q, ts - 1:ts, :]


def _rwkv(u, mu, w0, w2, a0, a2, g2, k_k, k_a, r_k, ln_w, ln_b, *, batch, seq, ts):
    w = RW_WIDTH
    ns = seq // ts
    nq = RW_SEQS if batch % RW_SEQS == 0 else 1
    zeros = lambda r, c: jnp.zeros((r, c), F32)
    w_lr = jnp.concatenate([jnp.concatenate([w2, zeros(RW_DECAY_RANK, w)], axis=1),
                            jnp.concatenate([zeros(RW_A_RANK, w), a2], axis=1)], axis=0)
    w_ab = jnp.concatenate([w0, a0]).reshape(1, 2 * w)
    vec = lambda p: p.reshape(1, w)
    nc = ts // RW_CHUNK
    c2 = 2 * RW_CHUNK
    scratch = [pltpu.VMEM((nq, 1, RW_COLS), F32), pltpu.VMEM((nq, w, w), F32),
               pltpu.VMEM((nq, ts, w), F32),
               pltpu.VMEM((nq, nc, c2, w), BF16), pltpu.VMEM((nq, nc, RW_CHUNK, w), BF16),
               pltpu.VMEM((nq, nc, RW_CHUNK, w), BF16), pltpu.VMEM((nq, nc, c2, w), F32),
               pltpu.VMEM((nq, nc, w, c2), BF16), pltpu.VMEM((nq, nc, w, LANES), F32)]
    out = pl.pallas_call(
        functools.partial(_rwkv_kernel, ts=ts),
        grid=(batch // nq, ns),
        in_specs=[pl.BlockSpec((nq, ts, RW_COLS), lambda b, i: (b, i, 0)),
                  _resident((1, RW_COLS)), _resident((LANES, 2 * w)), _resident((1, 2 * w)),
                  _resident((RW_GATE_RANK, w))] + [_resident((1, w))] * 5,
        out_specs=pl.BlockSpec((nq, ts, w), lambda b, i: (b, i, 0)),
        out_shape=jax.ShapeDtypeStruct((batch, seq, w), BF16),
        scratch_shapes=scratch,
        compiler_params=_cparams("parallel", "arbitrary"),
        name="rwkv7",
    )(u.reshape(batch, seq, RW_COLS), mu.reshape(1, RW_COLS), w_lr.astype(BF16), w_ab,
      g2.astype(BF16), vec(k_k), vec(k_a), vec(r_k), vec(ln_w), vec(ln_b))
    return out.reshape(batch * seq, w)


def _mem_kv_kernel(mem_ref, g_ref, w_ref, kv_ref):
    h = _rms(mem_ref[...], g_ref[...]).astype(BF16)
    kv_ref[...] = jnp.dot(h, w_ref[...], preferred_element_type=F32).astype(BF16)


def _mem_kv(mem, gain, w_kv, *, batch, mem_len):
    d = mem.shape[-1]
    n = w_kv.shape[1]
    return pl.pallas_call(
        _mem_kv_kernel,
        grid=(batch,),
        in_specs=[pl.BlockSpec((mem_len, d), lambda b: (b, 0)), _resident((1, d)),
                  _resident((d, n))],
        out_specs=pl.BlockSpec((mem_len, n), lambda b: (b, 0)),
        out_shape=jax.ShapeDtypeStruct((batch * mem_len, n), BF16),
        compiler_params=_cparams("parallel"),
        name="mem_kv",
    )(mem.reshape(batch * mem_len, d), gain.reshape(1, d), w_kv.astype(BF16))


def _pick(n, pref):
    t = min(n, pref)
    while n % t:
        t //= 2
    return t


def kernel(x, mem, positions, ffn1_norm, ffn1_w1, ffn1_w3, ffn1_w2, mix_norm, w_in, gate_bias, da_lambda_q1, da_lambda_k1, da_lambda_q2, da_lambda_k2, da_subln, rw_mu, rw_w0, rw_w2, rw_a0, rw_a2, rw_g2, rw_k_k, rw_k_a, rw_r_k, rw_ln_w, rw_ln_b, mem_norm, w_mem_kv, w_up_a, w_up_b, w_up_c, w_out, ffn2_norm, ffn2_w1, ffn2_w3, ffn2_w2, final_norm):
    batch, seq, d = x.shape
    mem_len = mem.shape[1]
    depth = ffn1_norm.shape[0]
    t = batch * seq
    tm = _pick(seq, 512)
    tq = _pick(seq, 2048)
    ts = _pick(seq, 512)

    xf = x.reshape(t, d)
    for l in range(depth):
        lam_init = 0.8 - 0.6 * math.exp(-0.3 * l)
        x1, h = _ffn_first(xf, ffn1_norm[l], ffn1_w1[l], ffn1_w3[l], ffn1_w2[l], mix_norm[l],
                           tm=tm)
        kv = _mem_kv(mem, mem_norm[l], w_mem_kv[l], batch=batch, mem_len=mem_len)
        q, k, v, u, o_c, gates = _in_proj(h, w_in[l], positions, gate_bias[l], kv,
                                          batch=batch, seq=seq, tm=tm)
        lam = _lambda(da_lambda_q1[l], da_lambda_k1[l], da_lambda_q2[l], da_lambda_k2[l], lam_init)
        o_a = _diff_attn(lam, q, k, v, da_subln[l], batch=batch, seq=seq, tq=tq,
                         out_scale=1.0 - lam_init)
        o_b = _rwkv(u, rw_mu[l], rw_w0[l], rw_w2[l], rw_a0[l], rw_a2[l], rw_g2[l], rw_k_k[l],
                    rw_k_a[l], rw_r_k[l], rw_ln_w[l], rw_ln_b[l], batch=batch, seq=seq, ts=ts)
        assert depth == 1, "stacked layers need the un-normalised output of the second half step"
        xf = _merge_ffn(x1, o_a, o_b, o_c, gates, w_up_a[l], w_up_b[l], w_up_c[l], w_out[l],
                        ffn2_norm[l], ffn2_w1[l], ffn2_w3[l], ffn2_w2[l], final_norm,
                        tm=tm)
    return xf.reshape(batch, seq, d)
```
